```python
import jax, jax.numpy as jnp
from jax import lax
import numpy as np

D_MODEL = 1024
BATCH = 16
SEQ = 256
DEPTH = 2
DEC_BATCH = 8
DEC_SEQ = 4096
PAST_LEN = 512

GRID_W = 64
N_HEADS = 4
DK_HEAD = 128
DV_HEAD = 128
GLA_K = N_HEADS * DK_HEAD
GLA_V = N_HEADS * DV_HEAD
GATE_RANK = 16
GATE_TAU = 16.0
CHUNK = 64
FNET_GROUPS = 4
FNET_GROUP_DIM = 64
FNET_W = FNET_GROUPS * FNET_GROUP_DIM
CONV_DIM = 256
CONV_WIDTH = 31
N_BRANCH = 3
D_FF = 2816
FFN_CONV_WIDTH = 3
NORM_EPS = 1e-6
POS_BASE = 10000.0
IN_SPLITS = (GLA_K, GLA_K, GLA_V, GLA_V, 2 * GATE_RANK, FNET_W, 2 * CONV_DIM, N_BRANCH * D_MODEL)
IN_W = sum(IN_SPLITS)
SPLIT_IDX = tuple(int(i) for i in np.cumsum(IN_SPLITS)[:-1])

kernel_name = 'hybrid_gla_fnet_conv_flow_step'


def rmsnorm(x, g):
    x32 = x.astype(jnp.float32)
    y = x32 * lax.rsqrt(jnp.mean(x32 * x32, axis=-1, keepdims=True) + NORM_EPS)
    return (y * g.astype(jnp.float32)).astype(x.dtype)


def layernorm(x, g, b):
    x32 = x.astype(jnp.float32)
    mu = jnp.mean(x32, axis=-1, keepdims=True)
    xc = x32 - mu
    y = xc * lax.rsqrt(jnp.mean(xc * xc, axis=-1, keepdims=True) + NORM_EPS)
    return (y * g.astype(jnp.float32) + b.astype(jnp.float32)).astype(x.dtype)


def dwconv(x, w, b):
    pad = (w.shape[0] - 1) // 2
    y = lax.conv_general_dilated(x, w[:, None, :].astype(x.dtype), window_strides=(1,),
                                 padding=[(pad, pad)], dimension_numbers=('NWC', 'WIO', 'NWC'),
                                 feature_group_count=x.shape[-1])
    return y + b.astype(x.dtype)


def pos_embed_2d(t, dtype):
    rows = t // GRID_W
    r = jnp.repeat(jnp.arange(rows, dtype=jnp.float32), GRID_W)
    col = jnp.tile(jnp.arange(GRID_W, dtype=jnp.float32), rows)
    nf = D_MODEL // 4
    omega = 1.0 / (POS_BASE ** (jnp.arange(nf, dtype=jnp.float32) / nf))
    ar = r[:, None] * omega
    ac = col[:, None] * omega
    return jnp.concatenate([jnp.sin(ar), jnp.cos(ar), jnp.sin(ac), jnp.cos(ac)], axis=-1).astype(dtype)


def gla_chunked(q, k, v, log_a, s0, strict):
    bsz, t, h, dk = q.shape
    dv = v.shape[-1]
    n = t // CHUNK
    f32 = jnp.float32
    q = q.astype(f32).reshape(bsz, n, CHUNK, h, dk)
    k = k.astype(f32).reshape(bsz, n, CHUNK, h, dk)
    v = v.astype(f32).reshape(bsz, n, CHUNK, h, dv)
    b = jnp.cumsum(log_a.astype(f32).reshape(bsz, n, CHUNK, h, dk), axis=2)
    b_last = b[:, :, -1]
    q_dec = q * jnp.exp(b)
    k_inv = k * jnp.exp(-b)
    k_end = k * jnp.exp(b_last[:, :, None] - b)
    mask = jnp.tri(CHUNK, k=-1 if strict else 0, dtype=bool)
    scores = jnp.einsum('bnihk,bnjhk->bnhij', q_dec, k_inv)
    scores = jnp.where(mask, scores, 0.0)
    o = jnp.einsum('bnhij,bnjhv->bnihv', scores, v)
    ds = jnp.einsum('bnjhk,bnjhv->bnhkv', k_end, v)

    def step(s, inp):
        decay, d = inp
        return jnp.exp(decay)[..., None] * s + d, s

    s_fin, s_prev = lax.scan(step, s0.astype(f32), (jnp.moveaxis(b_last, 1, 0), jnp.moveaxis(ds, 1, 0)))
    s_prev = jnp.moveaxis(s_prev, 0, 1)
    o = o + jnp.einsum('bnihk,bnhkv->bnihv', q_dec, s_prev)
    return o.reshape(bsz, t, h, dv), s_fin


def gla_bidir(q, k, v, log_a_f, log_a_b, s0_f, s0_b):
    o_f, s_f = gla_chunked(q, k, v, log_a_f, s0_f, False)
    rev = lambda a: a[:, ::-1]
    o_b, s_b = gla_chunked(rev(q), rev(k), rev(v), rev(log_a_b), s0_b, True)
    return o_f + rev(o_b), s_f, s_b


def trunk_layer(x, mod, s0, w_in, b_in, w_gate2, b_gate2, g_gla, w_br_gla, w_br_fft,
                conv_w, conv_b, conv_ln_g, conv_ln_b, w_br_conv, w_out, g_norm1, g_norm2,
                w_up, ffn_conv_w, ffn_conv_b, w_down):
    bsz, t, _ = x.shape
    f32 = jnp.float32
    sh1, sc1, ga1, sh2, sc2, ga2 = jnp.split(mod[:, None, :].astype(x.dtype), 6, axis=-1)
    h = rmsnorm(x, g_norm1) * (1.0 + sc1) + sh1
    z = h @ w_in + b_in
    q, k, v, g, lr, uf, uc, mg = jnp.split(z, SPLIT_IDX, axis=-1)

    lr = lr.reshape(bsz, t, 2, GATE_RANK).astype(f32)
    gate_logit = jnp.einsum('btdr,drk->btdk', lr, w_gate2.astype(f32)) + b_gate2.astype(f32)
    log_a = (jax.nn.log_sigmoid(gate_logit) / GATE_TAU).reshape(bsz, t, 2, N_HEADS, DK_HEAD)
    qh = q.reshape(bsz, t, N_HEADS, DK_HEAD) * (DK_HEAD ** -0.5)
    kh = k.reshape(bsz, t, N_HEADS, DK_HEAD)
    vh = v.reshape(bsz, t, N_HEADS, DV_HEAD)
    o, s_f, s_b = gla_bidir(qh, kh, vh, log_a[:, :, 0], log_a[:, :, 1], s0[:, 0], s0[:, 1])
    o = rmsnorm(o, g_gla).astype(x.dtype).reshape(bsz, t, GLA_V) * jax.nn.silu(g)
    y_a = o @ w_br_gla

    uf4 = uf.reshape(bsz, t, FNET_GROUPS, FNET_GROUP_DIM).astype(f32)
    y_f = jnp.fft.fftn(uf4, axes=(1, 3), norm='ortho').real.astype(x.dtype).reshape(bsz, t, FNET_W)
    y_b = y_f @ w_br_fft

    ua, ug = jnp.split(uc, 2, axis=-1)
    u = dwconv(ua * jax.nn.sigmoid(ug), conv_w, conv_b)
    u = jax.nn.silu(layernorm(u, conv_ln_g, conv_ln_b))
    y_c = u @ w_br_conv

    mg = jax.nn.sigmoid(mg.reshape(bsz, t, N_BRANCH, D_MODEL))
    merged = mg[:, :, 0] * y_a + mg[:, :, 1] * y_b + mg[:, :, 2] * y_c
    x = x + ga1 * (merged @ w_out)

    h2 = rmsnorm(x, g_norm2) * (1.0 + sc2) + sh2
    up = dwconv(h2 @ w_up, ffn_conv_w, ffn_conv_b)
    gate, val = jnp.split(up, 2, axis=-1)
    x = x + ga2 * ((jax.nn.silu(gate) * val) @ w_down)
    return x, jnp.stack([s_f, s_b], axis=1)


def setup_inputs(seed: int = 0) -> dict:
    key = jax.random.key(seed)
    ks = jax.random.split(key, 32)
    f32 = jnp.float32
    nrm = lambda kk, shape, s: jax.random.normal(kk, shape, f32) * s
    D = D_MODEL
    return {
        'x_prompt': nrm(ks[0], (BATCH, SEQ, D), 1.0),
        'x_sample': nrm(ks[1], (DEC_BATCH, DEC_SEQ, D), 1.0),
        'state_gla': nrm(ks[2], (DEC_BATCH, DEPTH, 2, N_HEADS, DK_HEAD, DV_HEAD), 1.0),
        'c': nrm(ks[3], (DEC_BATCH, D), 1.0),
        'c_ctx': nrm(ks[4], (D,), 1.0),
        'w_ada': nrm(ks[5], (DEPTH, D, 6 * D), 0.5 * D ** -0.5),
        'b_ada': nrm(ks[6], (DEPTH, 6 * D), 0.02),
        'g_norm1': 1.0 + nrm(ks[7], (DEPTH, D), 0.02),
        'w_in': nrm(ks[8], (DEPTH, D, IN_W), D ** -0.5),
        'b_in': nrm(ks[9], (DEPTH, IN_W), 0.02),
        'w_gate2': nrm(ks[10], (DEPTH, 2, GATE_RANK, GLA_K), GATE_RANK ** -0.5),
        'b_gate2': nrm(ks[11], (DEPTH, 2, GLA_K), 0.02),
        'g_gla': 1.0 + nrm(ks[12], (DEPTH, DV_HEAD), 0.02),
        'w_br_gla': nrm(ks[13], (DEPTH, GLA_V, D), GLA_V ** -0.5),
        'w_br_fft': nrm(ks[14], (DEPTH, FNET_W, D), FNET_W ** -0.5),
        'conv_w': nrm(ks[15], (DEPTH, CONV_WIDTH, CONV_DIM), CONV_WIDTH ** -0.5),
        'conv_b': nrm(ks[16], (DEPTH, CONV_DIM), 0.02),
        'conv_ln_g': 1.0 + nrm(ks[17], (DEPTH, CONV_DIM), 0.02),
        'conv_ln_b': nrm(ks[18], (DEPTH, CONV_DIM), 0.02),
        'w_br_conv': nrm(ks[19], (DEPTH, CONV_DIM, D), CONV_DIM ** -0.5),
        'w_out': nrm(ks[20], (DEPTH, D, D), D ** -0.5),
        'g_norm2': 1.0 + nrm(ks[21], (DEPTH, D), 0.02),
        'w_up': nrm(ks[22], (DEPTH, D, 2 * D_FF), D ** -0.5),
        'ffn_conv_w': nrm(ks[23], (DEPTH, FFN_CONV_WIDTH, 2 * D_FF), FFN_CONV_WIDTH ** -0.5),
        'ffn_conv_b': nrm(ks[24], (DEPTH, 2 * D_FF), 0.02),
        'w_down': nrm(ks[25], (DEPTH, D_FF, D), D_FF ** -0.5),
        'g_final': 1.0 + nrm(ks[26], (D,), 0.02),
    }


def reference(x_prompt, x_sample, state_gla, c, c_ctx, w_ada, b_ada, g_norm1, w_in, b_in,
              w_gate2, b_gate2, g_gla, w_br_gla, w_br_fft, conv_w, conv_b, conv_ln_g, conv_ln_b,
              w_br_conv, w_out, g_norm2, w_up, ffn_conv_w, ffn_conv_b, w_down, g_final):
    xp = x_prompt
    t_lat = x_sample.shape[1]
    xs = x_sample + pos_embed_2d(t_lat, x_sample.dtype)[None]
    ctx_act = jax.nn.silu(c_ctx)[None]
    lat_act = jax.nn.silu(c)
    zero_state = jnp.zeros((xp.shape[0], 2, N_HEADS, DK_HEAD, DV_HEAD), jnp.float32)
    new_states = []
    for l in range(DEPTH):
        lw = (w_in[l], b_in[l], w_gate2[l], b_gate2[l], g_gla[l], w_br_gla[l], w_br_fft[l],
              conv_w[l], conv_b[l], conv_ln_g[l], conv_ln_b[l], w_br_conv[l], w_out[l],
              g_norm1[l], g_norm2[l], w_up[l], ffn_conv_w[l], ffn_conv_b[l], w_down[l])
        mod_ctx = ctx_act @ w_ada[l] + b_ada[l]
        xp, s_ctx = trunk_layer(xp, mod_ctx, zero_state, *lw)
        new_states.append(s_ctx.astype(x_prompt.dtype))
        mod_lat = lat_act @ w_ada[l] + b_ada[l]
        xs, _ = trunk_layer(xs, mod_lat, state_gla[:, l], *lw)
    y_prompt = rmsnorm(xp, g_final)
    y_sample = rmsnorm(xs, g_final)
    new_state_gla = jnp.stack(new_states, axis=1)
    return (y_prompt, y_sample, new_state_gla)
```

```python
import functools

import numpy as np
import jax
import jax.numpy as jnp
from jax import lax
from jax.experimental import pallas as pl
from jax.experimental.pallas import tpu as pltpu

F32 = jnp.float32
BF16 = jnp.bfloat16

GRID_W = 64
N_HEADS = 4
DK_HEAD = 128
DV_HEAD = 128
GLA_K = N_HEADS * DK_HEAD
GLA_V = N_HEADS * DV_HEAD
GATE_RANK = 16
GATE_TAU = 16.0
CHUNK = 64
FNET_GROUPS = 4
FNET_GROUP_DIM = 64
FNET_W = FNET_GROUPS * FNET_GROUP_DIM
CONV_DIM = 256
CONV_WIDTH = 31
N_BRANCH = 3
FFN_CONV_WIDTH = 3
NORM_EPS = 1e-6
POS_BASE = 10000.0

LANES = 128
SUBLANES = 8
LR_PAD = LANES
CONV_HALO = 16
FFN_HALO = SUBLANES
TOKEN_TILE = 512
GLA_TILE = 256
DFT_TILE = 1024
FFN_COLS = 256
VMEM_LIMIT = 56 * 1024 * 1024


def _cparams(sem):
    return pltpu.CompilerParams(dimension_semantics=sem, vmem_limit_bytes=VMEM_LIMIT)


def _resident(shape):
    nd = len(shape)
    return pl.BlockSpec(shape, lambda *_: (0,) * nd, pipeline_mode=pl.Buffered(1))


def _dot(a, b):
    return jnp.dot(a, b, preferred_element_type=F32)


def _dot_nt(a, b):
    return lax.dot_general(a, b, (((1,), (1,)), ((), ())), preferred_element_type=F32)


def _dot_tn(a, b):
    return lax.dot_general(a, b, (((0,), (0,)), ((), ())), preferred_element_type=F32)


def _silu(x):
    return x * jax.nn.sigmoid(x)


def _rms(x, g):
    return x * lax.rsqrt(jnp.mean(x * x, axis=-1, keepdims=True) + NORM_EPS) * g


def _split3(x):
    hi = x.astype(BF16)
    r = x - hi.astype(F32)
    mid = r.astype(BF16)
    lo = (r - mid.astype(F32)).astype(BF16)
    return hi, mid, lo


def _mod_kernel(c_ref, w_ref, b_ref, o_ref):
    act = _silu(c_ref[...]).astype(BF16)
    o_ref[0] = _dot(act, w_ref[0].astype(BF16)) + b_ref[0]


def _modulation(cvec, w_ada, b_ada):
    depth, d, n = w_ada.shape
    rows = cvec.shape[0]
    tn = 1024
    return pl.pallas_call(
        _mod_kernel,
        out_shape=jax.ShapeDtypeStruct((depth, rows, n), F32),
        grid=(depth, n // tn),
        in_specs=[pl.BlockSpec((rows, d), lambda l, j: (0, 0)),
                  pl.BlockSpec((1, d, tn), lambda l, j: (l, 0, j)),
                  pl.BlockSpec((1, 1, tn), lambda l, j: (l, 0, j))],
        out_specs=pl.BlockSpec((1, rows, tn), lambda l, j: (l, 0, j)),
        compiler_params=_cparams(("arbitrary", "arbitrary")),
        name="modulation",
    )(cvec, w_ada, b_ada.reshape(depth, 1, n))


def _inproj_kernel(*refs, add_pos):
    if add_pos:
        (x_ref, pos_ref, mod_ref, g1_ref, w_ref, b_ref, csh_ref, csl_ref,
         qkv_ref, lr_ref, fa_ref, fb_ref, u0_ref, xs_ref) = refs
        x = x_ref[0] + pos_ref[...]
        xs_ref[0] = x
    else:
        (x_ref, mod_ref, g1_ref, w_ref, b_ref, csh_ref, csl_ref,
         qkv_ref, lr_ref, fa_ref, fb_ref, u0_ref) = refs
        x = x_ref[0]
    h = (_rms(x, g1_ref[...]) * (1.0 + mod_ref[0, 1:2, :]) + mod_ref[0, 0:1, :]).astype(BF16)

    c0 = 0
    for i in range(3):
        z = _dot(h, w_ref[:, c0:c0 + GLA_K]) + b_ref[:, c0:c0 + GLA_K]
        if i == 0:
            z = z * (DK_HEAD ** -0.5)
        qkv_ref[0, :, c0:c0 + GLA_K] = z.astype(BF16)
        c0 += GLA_K
    lr_ref[0] = (_dot(h, w_ref[:, c0:c0 + LR_PAD]) + b_ref[:, c0:c0 + LR_PAD]).astype(BF16)
    c0 += LR_PAD
    uf = _dot(h, w_ref[:, c0:c0 + FNET_W]) + b_ref[:, c0:c0 + FNET_W]
    c0 += FNET_W
    uf_hi = uf.astype(BF16)
    uf_lo = (uf - uf_hi.astype(F32)).astype(BF16)
    ab = _dot(uf_hi, csh_ref[...]) + _dot(uf_lo, csh_ref[...]) + _dot(uf_hi, csl_ref[...])
    fa_ref[0] = ab[:, :FNET_W].astype(BF16)
    fb_ref[0] = ab[:, FNET_W:].astype(BF16)
    uc = _dot(h, w_ref[:, c0:c0 + 2 * CONV_DIM]) + b_ref[:, c0:c0 + 2 * CONV_DIM]
    u0_ref[0] = uc[:, :CONV_DIM] * jax.nn.sigmoid(uc[:, CONV_DIM:])


def _inproj(x, pos, mod, g1, w1, b1, cs_hi, cs_lo):
    bsz, t, d = x.shape
    tm = min(t, TOKEN_TILE)
    n1 = w1.shape[1]
    add_pos = pos is not None
    mod_b = mod.shape[0]
    tok = lambda w: pl.BlockSpec((1, tm, w), lambda b, j: (b, j, 0))
    in_specs = [tok(d)]
    args = [x]
    if add_pos:
        in_specs.append(pl.BlockSpec((tm, d), lambda b, j: (j, 0)))
        args.append(pos)
    in_specs += [pl.BlockSpec((1, 6, d), (lambda b, j: (b, 0, 0)) if mod_b > 1 else (lambda b, j: (0, 0, 0))),
                 _resident((1, d)), _resident((d, n1)), _resident((1, n1)),
                 _resident(cs_hi.shape), _resident(cs_lo.shape)]
    args += [mod, g1, w1, b1, cs_hi, cs_lo]
    out_shape = [jax.ShapeDtypeStruct((bsz, t, 3 * GLA_K), BF16),
                 jax.ShapeDtypeStruct((bsz, t, LR_PAD), BF16),
                 jax.ShapeDtypeStruct((bsz, t, FNET_W), BF16),
                 jax.ShapeDtypeStruct((bsz, t, FNET_W), BF16),
                 jax.ShapeDtypeStruct((bsz, t, CONV_DIM), F32)]
    out_specs = [tok(3 * GLA_K), tok(LR_PAD), tok(FNET_W), tok(FNET_W), tok(CONV_DIM)]
    if add_pos:
        out_shape.append(jax.ShapeDtypeStruct((bsz, t, d), F32))
        out_specs.append(tok(d))
    return pl.pallas_call(
        functools.partial(_inproj_kernel, add_pos=add_pos),
        out_shape=out_shape, grid=(bsz, t // tm), in_specs=in_specs, out_specs=out_specs,
        compiler_params=_cparams(("parallel", "parallel")),
        name="inproj",
    )(*args)


def _gla_kernel(*refs, reverse, has_s0, add_ob, emit_state):
    refs = list(refs)
    qkv_ref, lr_ref, wg_ref, bg_ref = refs[:4]
    pos = 4
    s0_ref = ob_ref = sfin_ref = None
    if has_s0:
        s0_ref = refs[pos]; pos += 1
    if add_ob:
        ob_ref = refs[pos]; pos += 1
    o_ref = refs[pos]; pos += 1
    if emit_state:
        sfin_ref = refs[pos]; pos += 1
    st_ref = refs[pos]

    j = pl.program_id(1)

    @pl.when(j == 0)
    def _():
        for hh in range(N_HEADS):
            if has_s0:
                st_ref[hh] = s0_ref[0, hh].T
            else:
                st_ref[hh] = jnp.zeros((DV_HEAD, DK_HEAD), F32)

    row = lax.broadcasted_iota(jnp.int32, (CHUNK, CHUNK), 0)
    col = lax.broadcasted_iota(jnp.int32, (CHUNK, CHUNK), 1)
    if reverse:
        cum = (col >= row).astype(BF16)
        keep = col > row
    else:
        cum = (col <= row).astype(BF16)
        keep = col <= row
    n_chunks = qkv_ref.shape[1] // CHUNK
    order = range(n_chunks - 1, -1, -1) if reverse else range(n_chunks)
    for c in order:
        rows = slice(c * CHUNK, (c + 1) * CHUNK)
        logit = _dot(lr_ref[0, rows, :], wg_ref[...]) + bg_ref[...]
        log_a = (jnp.minimum(logit, 0.0) - jnp.log1p(jnp.exp(-jnp.abs(logit)))) * (1.0 / GATE_TAU)
        hi, mid, lo = _split3(log_a)
        b = _dot(cum, hi) + _dot(cum, mid) + _dot(cum, lo)
        b_last = b[0:1, :] if reverse else b[CHUNK - 1:CHUNK, :]
        e_b = jnp.exp(b)
        e_nb = jnp.exp(-b)
        e_end = jnp.exp(b_last - b)
        dec = jnp.exp(b_last)
        for hh in range(N_HEADS):
            cs = slice(hh * DK_HEAD, (hh + 1) * DK_HEAD)
            q = qkv_ref[0, rows, hh * DK_HEAD:(hh + 1) * DK_HEAD].astype(F32)
            k = qkv_ref[0, rows, GLA_K + hh * DK_HEAD:GLA_K + (hh + 1) * DK_HEAD].astype(F32)
            v = qkv_ref[0, rows, 2 * GLA_K + hh * DV_HEAD:2 * GLA_K + (hh + 1) * DV_HEAD]
            q_dec = (q * e_b[:, cs]).astype(BF16)
            k_inv = (k * e_nb[:, cs]).astype(BF16)
            k_end = (k * e_end[:, cs]).astype(BF16)
            s = jnp.where(keep, _dot_nt(q_dec, k_inv), 0.0).astype(BF16)
            st = st_ref[hh]
            o = _dot(s, v) + _dot_nt(q_dec, st.astype(BF16))
            st_ref[hh] = st * dec[:, cs] + _dot_tn(v, k_end)
            if add_ob:
                o = o + ob_ref[0, rows, hh * DV_HEAD:(hh + 1) * DV_HEAD]
            o_ref[0, rows, hh * DV_HEAD:(hh + 1) * DV_HEAD] = o

    if emit_state:
        @pl.when(j == pl.num_programs(1) - 1)
        def _():
            for hh in range(N_HEADS):
                sfin_ref[0, hh] = st_ref[hh].T


def _gla(qkv, lr, wg, bg, s0, ob, *, reverse, emit_state):
    bsz, t, _ = qkv.shape
    tc = min(t, GLA_TILE)
    nblk = t // tc
    blk = (lambda b, j: (b, nblk - 1 - j, 0)) if reverse else (lambda b, j: (b, j, 0))
    in_specs = [pl.BlockSpec((1, tc, 3 * GLA_K), blk), pl.BlockSpec((1, tc, LR_PAD), blk),
                _resident(wg.shape), _resident(bg.shape)]
    args = [qkv, lr, wg, bg]
    state_spec = pl.BlockSpec((1, N_HEADS, DK_HEAD, DV_HEAD), lambda b, j: (b, 0, 0, 0))
    if s0 is not None:
        in_specs.append(state_spec)
        args.append(s0)
    if ob is not None:
        in_specs.append(pl.BlockSpec((1, tc, GLA_V), blk))
        args.append(ob)
    out_shape = [jax.ShapeDtypeStruct((bsz, t, GLA_V), F32)]
    out_specs = [pl.BlockSpec((1, tc, GLA_V), blk)]
    if emit_state:
        out_shape.append(jax.ShapeDtypeStruct((bsz, N_HEADS, DK_HEAD, DV_HEAD), F32))
        out_specs.append(state_spec)
    res = pl.pallas_call(
        functools.partial(_gla_kernel, reverse=reverse, has_s0=s0 is not None,
                          add_ob=ob is not None, emit_state=emit_state),
        out_shape=out_shape, grid=(bsz, nblk), in_specs=in_specs, out_specs=out_specs,
        scratch_shapes=[pltpu.VMEM((N_HEADS, DV_HEAD, DK_HEAD), F32)],
        compiler_params=_cparams(("parallel", "arbitrary")),
        name="gla_bwd" if reverse else "gla_fwd",
    )(*args)
    return res if emit_state else (res[0], None)


def _dft_kernel(ct_ref, st_ref, fa_ref, fb_ref, y_ref):
    y = _dot(ct_ref[...], fa_ref[0]) - _dot(st_ref[...], fb_ref[0])
    y_ref[0] = y.astype(BF16)


def _seq_dft(fa, fb, ct, st):
    bsz, t, w = fa.shape
    tk = min(t, DFT_TILE)
    mat = pl.BlockSpec((tk, t), lambda i, b: (i, 0))
    seq = pl.BlockSpec((1, t, w), lambda i, b: (b, 0, 0))
    return pl.pallas_call(
        _dft_kernel,
        out_shape=jax.ShapeDtypeStruct((bsz, t, w), BF16),
        grid=(t // tk, bsz), in_specs=[mat, mat, seq, seq],
        out_specs=pl.BlockSpec((1, tk, w), lambda i, b: (b, i, 0)),
        compiler_params=_cparams(("parallel", "parallel")),
        name="seq_dft",
    )(ct, st, fa, fb)


def _merge_kernel(x_ref, mod_ref, o_ref, yf_ref, u_ref, up_ref, un_ref,
                  g1_ref, w2_ref, b2_ref, ggla_ref, wa_ref, wb_ref,
                  cw_ref, cb_ref, lng_ref, lnb_ref, wc_ref, wo_ref,
                  out_ref, ext_ref):
    j = pl.program_id(1)
    tm = x_ref.shape[1]
    x = x_ref[0]
    h = (_rms(x, g1_ref[...]) * (1.0 + mod_ref[0, 1:2, :]) + mod_ref[0, 0:1, :]).astype(BF16)

    g = _dot(h, w2_ref[:, 0:GLA_V]) + b2_ref[:, 0:GLA_V]
    ggla = ggla_ref[...]
    for hh in range(N_HEADS):
        cs = slice(hh * DV_HEAD, (hh + 1) * DV_HEAD)
        ext_ref[0:tm, cs] = _rms(o_ref[0, :, cs], ggla) * _silu(g[:, cs])
    y_a = _dot(ext_ref[0:tm, :].astype(BF16), wa_ref[...])
    d = x.shape[1]
    merged = jax.nn.sigmoid(_dot(h, w2_ref[:, GLA_V:GLA_V + d]) + b2_ref[:, GLA_V:GLA_V + d]) * y_a

    y_b = _dot(yf_ref[0], wb_ref[...])
    c1 = GLA_V + d
    merged = merged + jax.nn.sigmoid(_dot(h, w2_ref[:, c1:c1 + d]) + b2_ref[:, c1:c1 + d]) * y_b

    prev = jnp.where(j > 0, up_ref[0], 0.0)
    nxt = jnp.where(j < pl.num_programs(1) - 1, un_ref[0], 0.0)
    ext_ref[0:CONV_HALO, 0:CONV_DIM] = prev
    ext_ref[CONV_HALO:CONV_HALO + tm, 0:CONV_DIM] = u_ref[0]
    ext_ref[CONV_HALO + tm:2 * CONV_HALO + tm, 0:CONV_DIM] = nxt
    pad = (CONV_WIDTH - 1) // 2
    acc = jnp.zeros((tm, CONV_DIM), F32)
    for k in range(CONV_WIDTH):
        off = CONV_HALO - pad + k
        acc = acc + ext_ref[off:off + tm, 0:CONV_DIM] * cw_ref[k:k + 1, :]
    u = acc + cb_ref[...]
    mu = jnp.mean(u, axis=-1, keepdims=True)
    uc = u - mu
    u = uc * lax.rsqrt(jnp.mean(uc * uc, axis=-1, keepdims=True) + NORM_EPS) * lng_ref[...] + lnb_ref[...]
    y_c = _dot(_silu(u).astype(BF16), wc_ref[...])
    c2 = GLA_V + 2 * d
    merged = merged + jax.nn.sigmoid(_dot(h, w2_ref[:, c2:c2 + d]) + b2_ref[:, c2:c2 + d]) * y_c

    out_ref[0] = x + mod_ref[0, 2:3, :] * _dot(merged.astype(BF16), wo_ref[...])


def _merge(x, mod, o, yf, u0, g1, w2, b2, ggla, wa, wb, cw, cb, lng, lnb, wc, wo):
    bsz, t, d = x.shape
    tm = min(t, TOKEN_TILE)
    nt = t // tm
    hb = tm // CONV_HALO
    nh = t // CONV_HALO
    mod_b = mod.shape[0]
    tok = lambda w: pl.BlockSpec((1, tm, w), lambda b, j: (b, j, 0))
    in_specs = [tok(d),
                pl.BlockSpec((1, 6, d), (lambda b, j: (b, 0, 0)) if mod_b > 1 else (lambda b, j: (0, 0, 0))),
                tok(GLA_V), tok(FNET_W), tok(CONV_DIM),
                pl.BlockSpec((1, CONV_HALO, CONV_DIM), lambda b, j: (b, jnp.maximum(j * hb - 1, 0), 0)),
                pl.BlockSpec((1, CONV_HALO, CONV_DIM), lambda b, j: (b, jnp.minimum((j + 1) * hb, nh - 1), 0))]
    consts = [g1, w2, b2, ggla, wa, wb, cw, cb, lng, lnb, wc, wo]
    in_specs += [_resident(a.shape) for a in consts]
    return pl.pallas_call(
        _merge_kernel,
        out_shape=jax.ShapeDtypeStruct((bsz, t, d), F32),
        grid=(bsz, nt), in_specs=in_specs, out_specs=tok(d),
        scratch_shapes=[pltpu.VMEM((tm + 2 * CONV_HALO, GLA_V), F32)],
        compiler_params=_cparams(("parallel", "parallel")),
        name="merge",
    )(x, mod, o, yf, u0, u0, u0, *consts)


def _ffn_kernel(x_ref, xp_ref, xn_ref, mod_ref, g2_ref, wu_ref, fw_ref, fb_ref, wd_ref, gf_ref,
                out_ref, he_ref, act_ref, *, final):
    j = pl.program_id(1)
    tm = x_ref.shape[1]
    d_ff = wd_ref.shape[0]
    g2 = g2_ref[...]
    sc = 1.0 + mod_ref[0, 4:5, :]
    sh = mod_ref[0, 3:4, :]
    x = x_ref[0]
    hp = jnp.where(j > 0, _rms(xp_ref[0], g2) * sc + sh, 0.0)
    hn = jnp.where(j < pl.num_programs(1) - 1, _rms(xn_ref[0], g2) * sc + sh, 0.0)
    he_ref[0:FFN_HALO, :] = hp
    he_ref[FFN_HALO:FFN_HALO + tm, :] = _rms(x, g2) * sc + sh
    he_ref[FFN_HALO + tm:2 * FFN_HALO + tm, :] = hn
    he = he_ref[...].astype(BF16)
    n_ext = tm + 2 * FFN_HALO

    def conv3(u, c0):
        w = fw_ref[:, c0:c0 + FFN_COLS]
        um = pltpu.roll(u, 1, 0)[FFN_HALO:FFN_HALO + tm]
        up = pltpu.roll(u, n_ext - 1, 0)[FFN_HALO:FFN_HALO + tm]
        return (um * w[0:1] + u[FFN_HALO:FFN_HALO + tm] * w[1:2] + up * w[2:3]
                + fb_ref[:, c0:c0 + FFN_COLS])

    for c in range(d_ff // FFN_COLS):
        c0 = c * FFN_COLS
        gate = conv3(_dot(he, wu_ref[:, c0:c0 + FFN_COLS]), c0)
        val = conv3(_dot(he, wu_ref[:, d_ff + c0:d_ff + c0 + FFN_COLS]), d_ff + c0)
        act_ref[:, c0:c0 + FFN_COLS] = (_silu(gate) * val).astype(BF16)
    y = x + mod_ref[0, 5:6, :] * _dot(act_ref[...], wd_ref[...])
    if final:
        y = _rms(y, gf_ref[...])
    out_ref[0] = y


def _ffn(x, mod, g2, wu, fw, fb, wd, gf, *, final):
    bsz, t, d = x.shape
    tm = min(t, TOKEN_TILE)
    nt = t // tm
    hb = tm // FFN_HALO
    nh = t // FFN_HALO
    d_ff = wd.shape[0]
    mod_b = mod.shape[0]
    tok = pl.BlockSpec((1, tm, d), lambda b, j: (b, j, 0))
    in_specs = [tok,
                pl.BlockSpec((1, FFN_HALO, d), lambda b, j: (b, jnp.maximum(j * hb - 1, 0), 0)),
                pl.BlockSpec((1, FFN_HALO, d), lambda b, j: (b, jnp.minimum((j + 1) * hb, nh - 1), 0)),
                pl.BlockSpec((1, 6, d), (lambda b, j: (b, 0, 0)) if mod_b > 1 else (lambda b, j: (0, 0, 0)))]
    consts = [g2, wu, fw, fb, wd, gf]
    in_specs += [_resident(a.shape) for a in consts]
    return pl.pallas_call(
        functools.partial(_ffn_kernel, final=final),
        out_shape=jax.ShapeDtypeStruct((bsz, t, d), F32),
        grid=(bsz, nt), in_specs=in_specs, out_specs=tok,
        scratch_shapes=[pltpu.VMEM((tm + 2 * FFN_HALO, d), F32),
                        pltpu.VMEM((tm, d_ff), BF16)],
        compiler_params=_cparams(("parallel", "parallel")),
        name="convffn",
    )(x, x, x, mod, *consts)


def _pos_embed_2d(t, d):
    rows = t // GRID_W
    r = jnp.repeat(jnp.arange(rows, dtype=F32), GRID_W)
    col = jnp.tile(jnp.arange(GRID_W, dtype=F32), rows)
    nf = d // 4
    omega = 1.0 / (POS_BASE ** (jnp.arange(nf, dtype=F32) / nf))
    ar = r[:, None] * omega
    ac = col[:, None] * omega
    return jnp.concatenate([jnp.sin(ar), jnp.cos(ar), jnp.sin(ac), jnp.cos(ac)], axis=-1)


def _channel_dft_mats():
    idx = np.arange(FNET_GROUP_DIM)
    ang = 2.0 * np.pi * ((idx[:, None] * idx[None, :]) % FNET_GROUP_DIM) / FNET_GROUP_DIM
    eye = np.eye(FNET_GROUPS)
    cs = np.concatenate([np.kron(eye, np.cos(ang)), np.kron(eye, np.sin(ang))], axis=1).astype(np.float32)
    cs = jnp.asarray(cs)
    hi = cs.astype(BF16)
    lo = (cs - hi.astype(F32)).astype(BF16)
    return hi, lo


def _seq_dft_mats(t):
    idx = jnp.arange(t, dtype=jnp.int32)
    ang = ((idx[:, None] * idx[None, :]) % t).astype(F32) * (2.0 * np.pi / t)
    scale = 1.0 / np.sqrt(float(t) * FNET_GROUP_DIM)
    return (jnp.cos(ang) * scale).astype(BF16), (jnp.sin(ang) * scale).astype(BF16)


def kernel(x_prompt, x_sample, state_gla, c, c_ctx, w_ada, b_ada, g_norm1, w_in, b_in, w_gate2, b_gate2, g_gla, w_br_gla, w_br_fft, conv_w, conv_b, conv_ln_g, conv_ln_b, w_br_conv, w_out, g_norm2, w_up, ffn_conv_w, ffn_conv_b, w_down, g_final):
    depth, d, _ = w_in.shape
    n_ctx = x_prompt.shape[0]
    n_lat = x_sample.shape[0]
    t_lat = x_sample.shape[1]

    rows = -(-(1 + n_lat) // SUBLANES) * SUBLANES
    cvec = jnp.concatenate([c_ctx[None], c, jnp.zeros((rows - 1 - n_lat, d), F32)], axis=0)
    mod_all = _modulation(cvec, w_ada, b_ada).reshape(depth, rows, 6, d)

    sp = np.cumsum([0, GLA_K, GLA_K, GLA_V, GLA_V, 2 * GATE_RANK, FNET_W, 2 * CONV_DIM, N_BRANCH * d])
    q0, k0, v0, g0, lr0, uf0, uc0, mg0, end = (int(s) for s in sp)

    def cols1(a):
        lr_pad = jnp.zeros(a.shape[:-1] + (LR_PAD - 2 * GATE_RANK,), a.dtype)
        return jnp.concatenate([a[..., q0:g0], a[..., lr0:uf0], lr_pad, a[..., uf0:mg0]], axis=-1)

    def cols2(a):
        return jnp.concatenate([a[..., g0:lr0], a[..., mg0:end]], axis=-1)

    w1 = cols1(w_in).astype(BF16)
    b1 = cols1(b_in)[:, None, :]
    w2 = cols2(w_in).astype(BF16)
    b2 = cols2(b_in)[:, None, :]
    wg = jnp.zeros((depth, 2, LR_PAD, GLA_K), F32)
    wg = wg.at[:, 0, 0:GATE_RANK].set(w_gate2[:, 0]).at[:, 1, GATE_RANK:2 * GATE_RANK].set(w_gate2[:, 1])
    wg = wg.astype(BF16)
    bg = b_gate2[:, :, None, :]
    wa = w_br_gla.astype(BF16)
    wb = w_br_fft.astype(BF16)
    wc = w_br_conv.astype(BF16)
    wo = w_out.astype(BF16)
    wu = w_up.astype(BF16)
    wd = w_down.astype(BF16)
    cs_hi, cs_lo = _channel_dft_mats()
    dft = {t: _seq_dft_mats(t) for t in {x_prompt.shape[1], t_lat}}
    pos = _pos_embed_2d(t_lat, d)

    def layer(x, pos_in, mod, s0, l, emit_state, final):
        res = _inproj(x, pos_in, mod, g_norm1[l][None], w1[l], b1[l], cs_hi, cs_lo)
        qkv, lr, fa, fb, u0 = res[:5]
        if pos_in is not None:
            x = res[5]
        s0_f = None if s0 is None else s0[:, 0]
        s0_b = None if s0 is None else s0[:, 1]
        ob, s_b = _gla(qkv, lr, wg[l, 1], bg[l, 1], s0_b, None, reverse=True, emit_state=emit_state)
        o, s_f = _gla(qkv, lr, wg[l, 0], bg[l, 0], s0_f, ob, reverse=False, emit_state=emit_state)
        ct, st = dft[x.shape[1]]
        yf = _seq_dft(fa, fb, ct, st)
        x = _merge(x, mod, o, yf, u0, g_norm1[l][None], w2[l], b2[l], g_gla[l][None], wa[l], wb[l],
                   conv_w[l], conv_b[l][None], conv_ln_g[l][None], conv_ln_b[l][None], wc[l], wo[l])
        x = _ffn(x, mod, g_norm2[l][None], wu[l], ffn_conv_w[l], ffn_conv_b[l][None], wd[l],
                 g_final[None], final=final)
        return x, s_f, s_b

    xp, xs = x_prompt, x_sample
    new_states = []
    for l in range(depth):
        final = l == depth - 1
        xp, s_f, s_b = layer(xp, None, mod_all[l, 0:1], None, l, True, final)
        new_states.append(jnp.stack([s_f, s_b], axis=1))
        xs, _, _ = layer(xs, pos if l == 0 else None, mod_all[l, 1:1 + n_lat], state_gla[:, l], l, False, final)
    return xp, xs, jnp.stack(new_states, axis=1)
```

```python
import functools

import numpy as np
import jax
import jax.numpy as jnp
from jax import lax
from jax.experimental import pallas as pl
from jax.experimental.pallas import tpu as pltpu

F32 = jnp.float32
BF16 = jnp.bfloat16

GRID_W = 64
N_HEADS = 4
DK_HEAD = 128
DV_HEAD = 128
GLA_K = N_HEADS * DK_HEAD
GLA_V = N_HEADS * DV_HEAD
GATE_RANK = 16
GATE_TAU = 16.0
CHUNK = 64
FNET_GROUPS = 4
FNET_GROUP_DIM = 64
FNET_W = FNET_GROUPS * FNET_GROUP_DIM
CONV_DIM = 256
CONV_WIDTH = 31
N_BRANCH = 3
FFN_CONV_WIDTH = 3
NORM_EPS = 1e-6
POS_BASE = 10000.0

LANES = 128
SUBLANES = 8
LR_PAD = LANES
CONV_HALO = 16
FFN_HALO = SUBLANES
TOKEN_TILE = 512
GLA_TILE = 512
GLA_SUB = 256
DFT_TILE = 1024
FFN_COLS = 256
VMEM_LIMIT = 56 * 1024 * 1024


def _cparams(sem):
    return pltpu.CompilerParams(dimension_semantics=sem, vmem_limit_bytes=VMEM_LIMIT)


def _resident(shape):
    nd = len(shape)
    return pl.BlockSpec(shape, lambda *_: (0,) * nd, pipeline_mode=pl.Buffered(1))


def _dot(a, b):
    return jnp.dot(a, b, preferred_element_type=F32)


def _dot_nt(a, b):
    return lax.dot_general(a, b, (((1,), (1,)), ((), ())), preferred_element_type=F32)


def _dot_tn(a, b):
    return lax.dot_general(a, b, (((0,), (0,)), ((), ())), preferred_element_type=F32)


def _silu(x):
    return x * jax.nn.sigmoid(x)


def _rms(x, g):
    return x * lax.rsqrt(jnp.mean(x * x, axis=-1, keepdims=True) + NORM_EPS) * g


def _split2(x):
    hi = x.astype(BF16)
    return hi, (x - hi.astype(F32)).astype(BF16)


def _mod_kernel(c_ref, w_ref, b_ref, o_ref):
    act = _silu(c_ref[...]).astype(BF16)
    o_ref[0] = _dot(act, w_ref[0].astype(BF16)) + b_ref[0]


def _modulation(cvec, w_ada, b_ada):
    depth, d, n = w_ada.shape
    rows = cvec.shape[0]
    tn = 1024
    return pl.pallas_call(
        _mod_kernel,
        out_shape=jax.ShapeDtypeStruct((depth, rows, n), F32),
        grid=(depth, n // tn),
        in_specs=[pl.BlockSpec((rows, d), lambda l, j: (0, 0)),
                  pl.BlockSpec((1, d, tn), lambda l, j: (l, 0, j)),
                  pl.BlockSpec((1, 1, tn), lambda l, j: (l, 0, j))],
        out_specs=pl.BlockSpec((1, rows, tn), lambda l, j: (l, 0, j)),
        compiler_params=_cparams(("arbitrary", "arbitrary")),
        name="modulation",
    )(cvec, w_ada, b_ada.reshape(depth, 1, n))


def _inproj_kernel(*refs, add_pos):
    if add_pos:
        (x_ref, pos_ref, mod_ref, g1_ref, w_ref, b_ref, csh_ref, csl_ref, wg_ref, bg_ref, cumf_ref, cumb_ref,
         gq_ref, v_ref, dec_ref, fa_ref, fb_ref, u0_ref, xs_ref) = refs
        x = x_ref[0] + pos_ref[...]
        xs_ref[0] = x
    else:
        (x_ref, mod_ref, g1_ref, w_ref, b_ref, csh_ref, csl_ref, wg_ref, bg_ref, cumf_ref, cumb_ref,
         gq_ref, v_ref, dec_ref, fa_ref, fb_ref, u0_ref) = refs
        x = x_ref[0]
    tm = x.shape[0]
    h = (_rms(x, g1_ref[...]) * (1.0 + mod_ref[0, 1:2, :]) + mod_ref[0, 0:1, :]).astype(BF16)

    c0 = 0
    zq = (_dot(h, w_ref[:, c0:c0 + GLA_K]) + b_ref[:, c0:c0 + GLA_K]) * (DK_HEAD ** -0.5)
    c0 += GLA_K
    zk = _dot(h, w_ref[:, c0:c0 + GLA_K]) + b_ref[:, c0:c0 + GLA_K]
    c0 += GLA_K
    v_ref[0] = (_dot(h, w_ref[:, c0:c0 + GLA_V]) + b_ref[:, c0:c0 + GLA_V]).astype(BF16)
    c0 += GLA_V
    lr = (_dot(h, w_ref[:, c0:c0 + LR_PAD]) + b_ref[:, c0:c0 + LR_PAD]).astype(BF16)
    c0 += LR_PAD
    logit = _dot(lr, wg_ref[...]) + bg_ref[...]
    log_a = (jnp.minimum(logit, 0.0) - jnp.log1p(jnp.exp(-jnp.abs(logit)))) * (1.0 / GATE_TAU)
    la_hi, la_lo = _split2(log_a)
    per_sub = GLA_SUB // CHUNK
    for sb in range(tm // GLA_SUB):
        rows = slice(sb * GLA_SUB, (sb + 1) * GLA_SUB)
        for dr in range(2):
            cols = slice(dr * GLA_K, (dr + 1) * GLA_K)
            cum = cumb_ref[...] if dr else cumf_ref[...]
            b = _dot(cum, la_hi[rows, cols]) + _dot(cum, la_lo[rows, cols])
            b_last = [b[c * CHUNK:c * CHUNK + 1] if dr else b[(c + 1) * CHUNK - 1:(c + 1) * CHUNK]
                      for c in range(per_sub)]
            b_last_rows = jnp.concatenate([jnp.broadcast_to(bl, (CHUNK, GLA_K)) for bl in b_last], axis=0)
            g0 = dr * 3 * GLA_K
            gq_ref[0, rows, g0:g0 + GLA_K] = (zq[rows] * jnp.exp(b)).astype(BF16)
            gq_ref[0, rows, g0 + GLA_K:g0 + 2 * GLA_K] = (zk[rows] * jnp.exp(-b)).astype(BF16)
            gq_ref[0, rows, g0 + 2 * GLA_K:g0 + 3 * GLA_K] = (zk[rows] * jnp.exp(b_last_rows - b)).astype(BF16)
            for c in range(per_sub):
                dec_ref[0, sb * per_sub + c:sb * per_sub + c + 1, cols] = jnp.exp(b_last[c])
    uf = _dot(h, w_ref[:, c0:c0 + FNET_W]) + b_ref[:, c0:c0 + FNET_W]
    c0 += FNET_W
    uf_hi = uf.astype(BF16)
    uf_lo = (uf - uf_hi.astype(F32)).astype(BF16)
    ab = _dot(uf_hi, csh_ref[...]) + _dot(uf_lo, csh_ref[...]) + _dot(uf_hi, csl_ref[...])
    fa_ref[0] = ab[:, :FNET_W].astype(BF16)
    fb_ref[0] = ab[:, FNET_W:].astype(BF16)
    uc = _dot(h, w_ref[:, c0:c0 + 2 * CONV_DIM]) + b_ref[:, c0:c0 + 2 * CONV_DIM]
    u0_ref[0] = uc[:, :CONV_DIM] * jax.nn.sigmoid(uc[:, CONV_DIM:])


def _inproj(x, pos, mod, g1, w1, b1, cs_hi, cs_lo, wg, bg, cum_f, cum_b):
    bsz, t, d = x.shape
    tm = min(t, TOKEN_TILE)
    add_pos = pos is not None
    mod_b = mod.shape[0]
    tok = lambda w: pl.BlockSpec((1, tm, w), lambda b, j: (b, j, 0))
    in_specs = [tok(d)]
    args = [x]
    if add_pos:
        in_specs.append(pl.BlockSpec((tm, d), lambda b, j: (j, 0)))
        args.append(pos)
    consts = [g1, w1, b1, cs_hi, cs_lo, wg, bg, cum_f, cum_b]
    in_specs += [pl.BlockSpec((1, 6, d), (lambda b, j: (b, 0, 0)) if mod_b > 1 else (lambda b, j: (0, 0, 0)))]
    in_specs += [_resident(a.shape) for a in consts]
    args += [mod] + consts
    out_shape = [jax.ShapeDtypeStruct((bsz, t, 6 * GLA_K), BF16),
                 jax.ShapeDtypeStruct((bsz, t, GLA_V), BF16),
                 jax.ShapeDtypeStruct((bsz, t // CHUNK, 2 * GLA_K), F32),
                 jax.ShapeDtypeStruct((bsz, t, FNET_W), BF16),
                 jax.ShapeDtypeStruct((bsz, t, FNET_W), BF16),
                 jax.ShapeDtypeStruct((bsz, t, CONV_DIM), F32)]
    out_specs = [tok(6 * GLA_K), tok(GLA_V),
                 pl.BlockSpec((1, tm // CHUNK, 2 * GLA_K), lambda b, j: (b, j, 0)),
                 tok(FNET_W), tok(FNET_W), tok(CONV_DIM)]
    if add_pos:
        out_shape.append(jax.ShapeDtypeStruct((bsz, t, d), F32))
        out_specs.append(tok(d))
    return pl.pallas_call(
        functools.partial(_inproj_kernel, add_pos=add_pos),
        out_shape=out_shape, grid=(bsz, t // tm), in_specs=in_specs, out_specs=out_specs,
        compiler_params=_cparams(("parallel", "parallel")),
        name="inproj",
    )(*args)


def _gla_kernel(*refs, reverse, has_s0, add_ob, emit_state):
    refs = list(refs)
    gq_ref, v_ref, dec_ref = refs[:3]
    pos = 3
    s0_ref = ob_ref = sfin_ref = None
    if has_s0:
        s0_ref = refs[pos]; pos += 1
    if add_ob:
        ob_ref = refs[pos]; pos += 1
    o_ref = refs[pos]; pos += 1
    if emit_state:
        sfin_ref = refs[pos]; pos += 1
    st_ref = refs[pos]

    j = pl.program_id(1)

    @pl.when(j == 0)
    def _():
        for hh in range(N_HEADS):
            if has_s0:
                st_ref[hh] = s0_ref[0, hh].T
            else:
                st_ref[hh] = jnp.zeros((DV_HEAD, DK_HEAD), F32)

    r = v_ref.shape[1]
    sub = min(r, GLA_SUB)
    per_sub = sub // CHUNK
    row = lax.broadcasted_iota(jnp.int32, (sub, sub), 0)
    col = lax.broadcasted_iota(jnp.int32, (sub, sub), 1)
    first = row - (row & (CHUNK - 1))
    if reverse:
        keep = (col > row) & (col < first + CHUNK)
    else:
        keep = (col <= row) & (col >= first)
    flip = (lambda n: range(n - 1, -1, -1)) if reverse else range
    for hh in range(N_HEADS):
        cs = slice(hh * DK_HEAD, (hh + 1) * DK_HEAD)
        st = st_ref[hh]
        for sb in flip(r // sub):
            rows = slice(sb * sub, (sb + 1) * sub)
            q_dec = gq_ref[0, rows, hh * DK_HEAD:(hh + 1) * DK_HEAD]
            k_inv = gq_ref[0, rows, GLA_K + hh * DK_HEAD:GLA_K + (hh + 1) * DK_HEAD]
            k_end = gq_ref[0, rows, 2 * GLA_K + hh * DK_HEAD:2 * GLA_K + (hh + 1) * DK_HEAD]
            v = v_ref[0, rows, cs]
            s = jnp.where(keep, _dot_nt(q_dec, k_inv), 0.0).astype(BF16)
            o_intra = _dot(s, v)
            parts = [None] * per_sub
            for c in flip(per_sub):
                cr = slice(c * CHUNK, (c + 1) * CHUNK)
                ci = sb * per_sub + c
                parts[c] = o_intra[cr] + _dot_nt(q_dec[cr], st.astype(BF16))
                st = st * dec_ref[0, ci:ci + 1, cs] + _dot_tn(v[cr], k_end[cr])
            o = jnp.concatenate(parts, axis=0)
            if add_ob:
                o = o + ob_ref[0, rows, cs]
            o_ref[0, rows, cs] = o
        st_ref[hh] = st

    if emit_state:
        @pl.when(j == pl.num_programs(1) - 1)
        def _():
            for hh in range(N_HEADS):
                sfin_ref[0, hh] = st_ref[hh].T


def _gla(gq, v, dec, s0, ob, *, reverse, emit_state):
    bsz, t, _ = v.shape
    tc = min(t, GLA_TILE)
    nblk = t // tc
    dr = 1 if reverse else 0
    blk = (lambda b, j: (b, nblk - 1 - j, 0)) if reverse else (lambda b, j: (b, j, 0))
    blk_dir = (lambda b, j: (b, nblk - 1 - j, dr)) if reverse else (lambda b, j: (b, j, dr))
    in_specs = [pl.BlockSpec((1, tc, 3 * GLA_K), blk_dir), pl.BlockSpec((1, tc, GLA_V), blk),
                pl.BlockSpec((1, tc // CHUNK, GLA_K), blk_dir)]
    args = [gq, v, dec]
    state_spec = pl.BlockSpec((1, N_HEADS, DK_HEAD, DV_HEAD), lambda b, j: (b, 0, 0, 0))
    if s0 is not None:
        in_specs.append(state_spec)
        args.append(s0)
    if ob is not None:
        in_specs.append(pl.BlockSpec((1, tc, GLA_V), blk))
        args.append(ob)
    out_shape = [jax.ShapeDtypeStruct((bsz, t, GLA_V), F32)]
    out_specs = [pl.BlockSpec((1, tc, GLA_V), blk)]
    if emit_state:
        out_shape.append(jax.ShapeDtypeStruct((bsz, N_HEADS, DK_HEAD, DV_HEAD), F32))
        out_specs.append(state_spec)
    res = pl.pallas_call(
        functools.partial(_gla_kernel, reverse=reverse, has_s0=s0 is not None,
                          add_ob=ob is not None, emit_state=emit_state),
        out_shape=out_shape, grid=(bsz, nblk), in_specs=in_specs, out_specs=out_specs,
        scratch_shapes=[pltpu.VMEM((N_HEADS, DV_HEAD, DK_HEAD), F32)],
        compiler_params=_cparams(("parallel", "arbitrary")),
        name="gla_bwd" if reverse else "gla_fwd",
    )(*args)
    return res if emit_state else (res[0], None)


def _dft_kernel(ct_ref, st_ref, fa_ref, fb_ref, y_ref):
    y = _dot(ct_ref[...], fa_ref[0]) - _dot(st_ref[...], fb_ref[0])
    y_ref[0] = y.astype(BF16)


def _seq_dft(fa, fb, ct, st):
    bsz, t, w = fa.shape
    tk = min(t, DFT_TILE)
    mat = pl.BlockSpec((tk, t), lambda i, b: (i, 0))
    seq = pl.BlockSpec((1, t, w), lambda i, b: (b, 0, 0))
    return pl.pallas_call(
        _dft_kernel,
        out_shape=jax.ShapeDtypeStruct((bsz, t, w), BF16),
        grid=(t // tk, bsz), in_specs=[mat, mat, seq, seq],
        out_specs=pl.BlockSpec((1, tk, w), lambda i, b: (b, i, 0)),
        compiler_params=_cparams(("parallel", "parallel")),
        name="seq_dft",
    )(ct, st, fa, fb)


def _merge_kernel(x_ref, mod_ref, o_ref, yf_ref, u_ref, up_ref, un_ref,
                  g1_ref, w2_ref, b2_ref, ggla_ref, wa_ref, wb_ref,
                  cw_ref, cb_ref, lng_ref, lnb_ref, wc_ref, wo_ref,
                  out_ref, ext_ref):
    j = pl.program_id(1)
    tm = x_ref.shape[1]
    x = x_ref[0]
    h = (_rms(x, g1_ref[...]) * (1.0 + mod_ref[0, 1:2, :]) + mod_ref[0, 0:1, :]).astype(BF16)

    g = _dot(h, w2_ref[:, 0:GLA_V]) + b2_ref[:, 0:GLA_V]
    ggla = ggla_ref[...]
    for hh in range(N_HEADS):
        cs = slice(hh * DV_HEAD, (hh + 1) * DV_HEAD)
        ext_ref[0:tm, cs] = _rms(o_ref[0, :, cs], ggla) * _silu(g[:, cs])
    y_a = _dot(ext_ref[0:tm, :].astype(BF16), wa_ref[...])
    d = x.shape[1]
    merged = jax.nn.sigmoid(_dot(h, w2_ref[:, GLA_V:GLA_V + d]) + b2_ref[:, GLA_V:GLA_V + d]) * y_a

    y_b = _dot(yf_ref[0], wb_ref[...])
    c1 = GLA_V + d
    merged = merged + jax.nn.sigmoid(_dot(h, w2_ref[:, c1:c1 + d]) + b2_ref[:, c1:c1 + d]) * y_b

    prev = jnp.where(j > 0, up_ref[0], 0.0)
    nxt = jnp.where(j < pl.num_programs(1) - 1, un_ref[0], 0.0)
    ext_ref[0:CONV_HALO, 0:CONV_DIM] = prev
    ext_ref[CONV_HALO:CONV_HALO + tm, 0:CONV_DIM] = u_ref[0]
    ext_ref[CONV_HALO + tm:2 * CONV_HALO + tm, 0:CONV_DIM] = nxt
    pad = (CONV_WIDTH - 1) // 2
    acc = jnp.zeros((tm, CONV_DIM), F32)
    for k in range(CONV_WIDTH):
        off = CONV_HALO - pad + k
        acc = acc + ext_ref[off:off + tm, 0:CONV_DIM] * cw_ref[k:k + 1, :]
    u = acc + cb_ref[...]
    mu = jnp.mean(u, axis=-1, keepdims=True)
    uc = u - mu
    u = uc * lax.rsqrt(jnp.mean(uc * uc, axis=-1, keepdims=True) + NORM_EPS) * lng_ref[...] + lnb_ref[...]
    y_c = _dot(_silu(u).astype(BF16), wc_ref[...])
    c2 = GLA_V + 2 * d
    merged = merged + jax.nn.sigmoid(_dot(h, w2_ref[:, c2:c2 + d]) + b2_ref[:, c2:c2 + d]) * y_c

    out_ref[0] = x + mod_ref[0, 2:3, :] * _dot(merged.astype(BF16), wo_ref[...])


def _merge(x, mod, o, yf, u0, g1, w2, b2, ggla, wa, wb, cw, cb, lng, lnb, wc, wo):
    bsz, t, d = x.shape
    tm = min(t, TOKEN_TILE)
    nt = t // tm
    hb = tm // CONV_HALO
    nh = t // CONV_HALO
    mod_b = mod.shape[0]
    tok = lambda w: pl.BlockSpec((1, tm, w), lambda b, j: (b, j, 0))
    in_specs = [tok(d),
                pl.BlockSpec((1, 6, d), (lambda b, j: (b, 0, 0)) if mod_b > 1 else (lambda b, j: (0, 0, 0))),
                tok(GLA_V), tok(FNET_W), tok(CONV_DIM),
                pl.BlockSpec((1, CONV_HALO, CONV_DIM), lambda b, j: (b, jnp.maximum(j * hb - 1, 0), 0)),
                pl.BlockSpec((1, CONV_HALO, CONV_DIM), lambda b, j: (b, jnp.minimum((j + 1) * hb, nh - 1), 0))]
    consts = [g1, w2, b2, ggla, wa, wb, cw, cb, lng, lnb, wc, wo]
    in_specs += [_resident(a.shape) for a in consts]
    return pl.pallas_call(
        _merge_kernel,
        out_shape=jax.ShapeDtypeStruct((bsz, t, d), F32),
        grid=(bsz, nt), in_specs=in_specs, out_specs=tok(d),
        scratch_shapes=[pltpu.VMEM((tm + 2 * CONV_HALO, GLA_V), F32)],
        compiler_params=_cparams(("parallel", "parallel")),
        name="merge",
    )(x, mod, o, yf, u0, u0, u0, *consts)


def _ffn_kernel(x_ref, xp_ref, xn_ref, mod_ref, g2_ref, wu_ref, fw_ref, fb_ref, wd_ref, gf_ref,
                out_ref, he_ref, act_ref, *, final):
    j = pl.program_id(1)
    tm = x_ref.shape[1]
    d_ff = wd_ref.shape[0]
    g2 = g2_ref[...]
    sc = 1.0 + mod_ref[0, 4:5, :]
    sh = mod_ref[0, 3:4, :]
    x = x_ref[0]
    hp = jnp.where(j > 0, _rms(xp_ref[0], g2) * sc + sh, 0.0)
    hn = jnp.where(j < pl.num_programs(1) - 1, _rms(xn_ref[0], g2) * sc + sh, 0.0)
    he_ref[0:FFN_HALO, :] = hp
    he_ref[FFN_HALO:FFN_HALO + tm, :] = _rms(x, g2) * sc + sh
    he_ref[FFN_HALO + tm:2 * FFN_HALO + tm, :] = hn
    he = he_ref[...].astype(BF16)
    n_ext = tm + 2 * FFN_HALO

    def conv3(u, c0):
        w = fw_ref[:, c0:c0 + FFN_COLS]
        um = pltpu.roll(u, 1, 0)[FFN_HALO:FFN_HALO + tm]
        up = pltpu.roll(u, n_ext - 1, 0)[FFN_HALO:FFN_HALO + tm]
        return (um * w[0:1] + u[FFN_HALO:FFN_HALO + tm] * w[1:2] + up * w[2:3]
                + fb_ref[:, c0:c0 + FFN_COLS])

    for c in range(d_ff // FFN_COLS):
        c0 = c * FFN_COLS
        gate = conv3(_dot(he, wu_ref[:, c0:c0 + FFN_COLS]), c0)
        val = conv3(_dot(he, wu_ref[:, d_ff + c0:d_ff + c0 + FFN_COLS]), d_ff + c0)
        act_ref[:, c0:c0 + FFN_COLS] = (_silu(gate) * val).astype(BF16)
    y = x + mod_ref[0, 5:6, :] * _dot(act_ref[...], wd_ref[...])
    if final:
        y = _rms(y, gf_ref[...])
    out_ref[0] = y


def _ffn(x, mod, g2, wu, fw, fb, wd, gf, *, final):
    bsz, t, d = x.shape
    tm = min(t, TOKEN_TILE)
    nt = t // tm
    hb = tm // FFN_HALO
    nh = t // FFN_HALO
    d_ff = wd.shape[0]
    mod_b = mod.shape[0]
    tok = pl.BlockSpec((1, tm, d), lambda b, j: (b, j, 0))
    in_specs = [tok,
                pl.BlockSpec((1, FFN_HALO, d), lambda b, j: (b, jnp.maximum(j * hb - 1, 0), 0)),
                pl.BlockSpec((1, FFN_HALO, d), lambda b, j: (b, jnp.minimum((j + 1) * hb, nh - 1), 0)),
                pl.BlockSpec((1, 6, d), (lambda b, j: (b, 0, 0)) if mod_b > 1 else (lambda b, j: (0, 0, 0)))]
    consts = [g2, wu, fw, fb, wd, gf]
    in_specs += [_resident(a.shape) for a in consts]
    return pl.pallas_call(
        functools.partial(_ffn_kernel, final=final),
        out_shape=jax.ShapeDtypeStruct((bsz, t, d), F32),
        grid=(bsz, nt), in_specs=in_specs, out_specs=tok,
        scratch_shapes=[pltpu.VMEM((tm + 2 * FFN_HALO, d), F32),
                        pltpu.VMEM((tm, d_ff), BF16)],
        compiler_params=_cparams(("parallel", "parallel")),
        name="convffn",
    )(x, x, x, mod, *consts)


def _pos_embed_2d(t, d):
    rows = t // GRID_W
    r = jnp.repeat(jnp.arange(rows, dtype=F32), GRID_W)
    col = jnp.tile(jnp.arange(GRID_W, dtype=F32), rows)
    nf = d // 4
    omega = 1.0 / (POS_BASE ** (jnp.arange(nf, dtype=F32) / nf))
    ar = r[:, None] * omega
    ac = col[:, None] * omega
    return jnp.concatenate([jnp.sin(ar), jnp.cos(ar), jnp.sin(ac), jnp.cos(ac)], axis=-1)


def _channel_dft_mats():
    idx = np.arange(FNET_GROUP_DIM)
    ang = 2.0 * np.pi * ((idx[:, None] * idx[None, :]) % FNET_GROUP_DIM) / FNET_GROUP_DIM
    eye = np.eye(FNET_GROUPS)
    cs = np.concatenate([np.kron(eye, np.cos(ang)), np.kron(eye, np.sin(ang))], axis=1).astype(np.float32)
    cs = jnp.asarray(cs)
    hi = cs.astype(BF16)
    lo = (cs - hi.astype(F32)).astype(BF16)
    return hi, lo


def _seq_dft_mats(t):
    idx = jnp.arange(t, dtype=jnp.int32)
    ang = ((idx[:, None] * idx[None, :]) % t).astype(F32) * (2.0 * np.pi / t)
    scale = 1.0 / np.sqrt(float(t) * FNET_GROUP_DIM)
    return (jnp.cos(ang) * scale).astype(BF16), (jnp.sin(ang) * scale).astype(BF16)


def kernel(x_prompt, x_sample, state_gla, c, c_ctx, w_ada, b_ada, g_norm1, w_in, b_in, w_gate2, b_gate2, g_gla, w_br_gla, w_br_fft, conv_w, conv_b, conv_ln_g, conv_ln_b, w_br_conv, w_out, g_norm2, w_up, ffn_conv_w, ffn_conv_b, w_down, g_final):
    depth, d, _ = w_in.shape
    n_ctx = x_prompt.shape[0]
    n_lat = x_sample.shape[0]
    t_lat = x_sample.shape[1]

    rows = -(-(1 + n_lat) // SUBLANES) * SUBLANES
    cvec = jnp.concatenate([c_ctx[None], c, jnp.zeros((rows - 1 - n_lat, d), F32)], axis=0)
    mod_all = _modulation(cvec, w_ada, b_ada).reshape(depth, rows, 6, d)

    sp = np.cumsum([0, GLA_K, GLA_K, GLA_V, GLA_V, 2 * GATE_RANK, FNET_W, 2 * CONV_DIM, N_BRANCH * d])
    q0, k0, v0, g0, lr0, uf0, uc0, mg0, end = (int(s) for s in sp)

    def cols1(a):
        lr_pad = jnp.zeros(a.shape[:-1] + (LR_PAD - 2 * GATE_RANK,), a.dtype)
        return jnp.concatenate([a[..., q0:g0], a[..., lr0:uf0], lr_pad, a[..., uf0:mg0]], axis=-1)

    def cols2(a):
        return jnp.concatenate([a[..., g0:lr0], a[..., mg0:end]], axis=-1)

    w1 = cols1(w_in).astype(BF16)
    b1 = cols1(b_in)[:, None, :]
    w2 = cols2(w_in).astype(BF16)
    b2 = cols2(b_in)[:, None, :]
    wg = jnp.zeros((depth, LR_PAD, 2 * GLA_K), F32)
    wg = wg.at[:, 0:GATE_RANK, 0:GLA_K].set(w_gate2[:, 0])
    wg = wg.at[:, GATE_RANK:2 * GATE_RANK, GLA_K:].set(w_gate2[:, 1]).astype(BF16)
    bg = b_gate2.reshape(depth, 1, 2 * GLA_K)
    ridx = np.arange(GLA_SUB)
    same = (ridx[:, None] // CHUNK) == (ridx[None, :] // CHUNK)
    cum_f = jnp.asarray(same & (ridx[None, :] <= ridx[:, None]), BF16)
    cum_b = jnp.asarray(same & (ridx[None, :] >= ridx[:, None]), BF16)
    wa = w_br_gla.astype(BF16)
    wb = w_br_fft.astype(BF16)
    wc = w_br_conv.astype(BF16)
    wo = w_out.astype(BF16)
    wu = w_up.astype(BF16)
    wd = w_down.astype(BF16)
    cs_hi, cs_lo = _channel_dft_mats()
    dft = {t: _seq_dft_mats(t) for t in {x_prompt.shape[1], t_lat}}
    pos = _pos_embed_2d(t_lat, d)

    def layer(x, pos_in, mod, s0, l, emit_state, final):
        res = _inproj(x, pos_in, mod, g_norm1[l][None], w1[l], b1[l], cs_hi, cs_lo, wg[l], bg[l], cum_f, cum_b)
        gq, v, dec, fa, fb, u0 = res[:6]
        if pos_in is not None:
            x = res[6]
        s0_f = None if s0 is None else s0[:, 0]
        s0_b = None if s0 is None else s0[:, 1]
        ob, s_b = _gla(gq, v, dec, s0_b, None, reverse=True, emit_state=emit_state)
        o, s_f = _gla(gq, v, dec, s0_f, ob, reverse=False, emit_state=emit_state)
        ct, st = dft[x.shape[1]]
        yf = _seq_dft(fa, fb, ct, st)
        x = _merge(x, mod, o, yf, u0, g_norm1[l][None], w2[l], b2[l], g_gla[l][None], wa[l], wb[l],
                   conv_w[l], conv_b[l][None], conv_ln_g[l][None], conv_ln_b[l][None], wc[l], wo[l])
        x = _ffn(x, mod, g_norm2[l][None], wu[l], ffn_conv_w[l], ffn_conv_b[l][None], wd[l],
                 g_final[None], final=final)
        return x, s_f, s_b

    xp, xs = x_prompt, x_sample
    new_states = []
    for l in range(depth):
        final = l == depth - 1
        xp, s_f, s_b = layer(xp, None, mod_all[l, 0:1], None, l, True, final)
        new_states.append(jnp.stack([s_f, s_b], axis=1))
        xs, _, _ = layer(xs, pos if l == 0 else None, mod_all[l, 1:1 + n_lat], state_gla[:, l], l, False, final)
    return xp, xs, jnp.stack(new_states, axis=1)
```

```python
import functools

import numpy as np
import jax
import jax.numpy as jnp
from jax import lax
from jax.experimental import pallas as pl
from jax.experimental.pallas import tpu as pltpu

F32 = jnp.float32
BF16 = jnp.bfloat16

GRID_W = 64
N_HEADS = 4
DK_HEAD = 128
DV_HEAD = 128
GLA_K = N_HEADS * DK_HEAD
GLA_V = N_HEADS * DV_HEAD
GATE_RANK = 16
GATE_TAU = 16.0
CHUNK = 64
FNET_GROUPS = 4
FNET_GROUP_DIM = 64
FNET_W = FNET_GROUPS * FNET_GROUP_DIM
CONV_DIM = 256
CONV_WIDTH = 31
N_BRANCH = 3
FFN_CONV_WIDTH = 3
NORM_EPS = 1e-6
POS_BASE = 10000.0

LANES = 128
SUBLANES = 8
LR_PAD = LANES
CONV_HALO = 16
FFN_HALO = SUBLANES
TOKEN_TILE = 512
GLA_TILE = 512
GLA_SUB = 256
DFT_TILE = 1024
DFT_K2 = 32
FFN_COLS = 256
VMEM_LIMIT = 56 * 1024 * 1024


def _cparams(sem):
    return pltpu.CompilerParams(dimension_semantics=sem, vmem_limit_bytes=VMEM_LIMIT)


def _resident(shape):
    nd = len(shape)
    return pl.BlockSpec(shape, lambda *_: (0,) * nd, pipeline_mode=pl.Buffered(1))


def _dot(a, b):
    return jnp.dot(a, b, preferred_element_type=F32)


def _dot_nt(a, b):
    return lax.dot_general(a, b, (((1,), (1,)), ((), ())), preferred_element_type=F32)


def _dot_tn(a, b):
    return lax.dot_general(a, b, (((0,), (0,)), ((), ())), preferred_element_type=F32)


def _silu(x):
    return x * jax.nn.sigmoid(x)


def _rms(x, g):
    return x * lax.rsqrt(jnp.mean(x * x, axis=-1, keepdims=True) + NORM_EPS) * g


def _split2(x):
    hi = x.astype(BF16)
    return hi, (x - hi.astype(F32)).astype(BF16)


def _mod_kernel(c_ref, w_ref, b_ref, o_ref):
    act = _silu(c_ref[...]).astype(BF16)
    o_ref[0] = _dot(act, w_ref[0].astype(BF16)) + b_ref[0]


def _modulation(cvec, w_ada, b_ada):
    depth, d, n = w_ada.shape
    rows = cvec.shape[0]
    tn = 1024
    return pl.pallas_call(
        _mod_kernel,
        out_shape=jax.ShapeDtypeStruct((depth, rows, n), F32),
        grid=(depth, n // tn),
        in_specs=[pl.BlockSpec((rows, d), lambda l, j: (0, 0)),
                  pl.BlockSpec((1, d, tn), lambda l, j: (l, 0, j)),
                  pl.BlockSpec((1, 1, tn), lambda l, j: (l, 0, j))],
        out_specs=pl.BlockSpec((1, rows, tn), lambda l, j: (l, 0, j)),
        compiler_params=_cparams(("arbitrary", "arbitrary")),
        name="modulation",
    )(cvec, w_ada, b_ada.reshape(depth, 1, n))


def _inproj_kernel(*refs, add_pos):
    if add_pos:
        (x_ref, pos_ref, mod_ref, g1_ref, w_ref, b_ref, csh_ref, csl_ref, wg_ref, bg_ref, cumf_ref, cumb_ref,
         gq_ref, v_ref, dec_ref, fa_ref, fb_ref, u0_ref, xs_ref) = refs
        x = x_ref[0] + pos_ref[...]
        xs_ref[0] = x
    else:
        (x_ref, mod_ref, g1_ref, w_ref, b_ref, csh_ref, csl_ref, wg_ref, bg_ref, cumf_ref, cumb_ref,
         gq_ref, v_ref, dec_ref, fa_ref, fb_ref, u0_ref) = refs
        x = x_ref[0]
    tm = x.shape[0]
    h = (_rms(x, g1_ref[...]) * (1.0 + mod_ref[0, 1:2, :]) + mod_ref[0, 0:1, :]).astype(BF16)

    def proj(c0, width):
        return _dot(h, w_ref[:, c0:c0 + width]) + b_ref[:, c0:c0 + width]

    q0, k0, v0, lr0 = 0, GLA_K, 2 * GLA_K, 2 * GLA_K + GLA_V
    uf0 = lr0 + LR_PAD
    uc0 = uf0 + FNET_W
    lr = proj(lr0, LR_PAD).astype(BF16)
    logit = _dot(lr, wg_ref[...]) + bg_ref[...]
    log_a = (jnp.minimum(logit, 0.0) - jnp.log(1.0 + jnp.exp(-jnp.abs(logit)))) * (1.0 / GATE_TAU)
    la_hi, la_lo = _split2(log_a)
    zq = proj(q0, GLA_K) * (DK_HEAD ** -0.5)
    zk = proj(k0, GLA_K)
    per_sub = GLA_SUB // CHUNK
    for sb in range(tm // GLA_SUB):
        rows = slice(sb * GLA_SUB, (sb + 1) * GLA_SUB)
        for dr in range(2):
            cols = slice(dr * GLA_K, (dr + 1) * GLA_K)
            cum = cumb_ref[...] if dr else cumf_ref[...]
            b = _dot(cum, la_hi[rows, cols]) + _dot(cum, la_lo[rows, cols])
            b_last = [b[c * CHUNK:c * CHUNK + 1] if dr else b[(c + 1) * CHUNK - 1:(c + 1) * CHUNK]
                      for c in range(per_sub)]
            b_last_rows = jnp.concatenate([jnp.broadcast_to(bl, (CHUNK, GLA_K)) for bl in b_last], axis=0)
            g0 = dr * 3 * GLA_K
            gq_ref[0, rows, g0:g0 + GLA_K] = (zq[rows] * jnp.exp(b)).astype(BF16)
            gq_ref[0, rows, g0 + GLA_K:g0 + 2 * GLA_K] = (zk[rows] * jnp.exp(-b)).astype(BF16)
            gq_ref[0, rows, g0 + 2 * GLA_K:g0 + 3 * GLA_K] = (zk[rows] * jnp.exp(b_last_rows - b)).astype(BF16)
            for c in range(per_sub):
                dec_ref[0, sb * per_sub + c:sb * per_sub + c + 1, cols] = jnp.exp(b_last[c])
    v_ref[0] = proj(v0, GLA_V).astype(BF16)
    uf_hi, uf_lo = _split2(proj(uf0, FNET_W))
    ab = _dot(uf_hi, csh_ref[...]) + _dot(uf_lo, csh_ref[...]) + _dot(uf_hi, csl_ref[...])
    fa_ref[0] = ab[:, :FNET_W]
    fb_ref[0] = ab[:, FNET_W:]
    uc = proj(uc0, 2 * CONV_DIM)
    u0_ref[0] = uc[:, :CONV_DIM] * jax.nn.sigmoid(uc[:, CONV_DIM:])


def _inproj(x, pos, mod, g1, w1, b1, cs_hi, cs_lo, wg, bg, cum_f, cum_b):
    bsz, t, d = x.shape
    tm = min(t, TOKEN_TILE)
    add_pos = pos is not None
    mod_b = mod.shape[0]
    tok = lambda w: pl.BlockSpec((1, tm, w), lambda b, j: (b, j, 0))
    in_specs = [tok(d)]
    args = [x]
    if add_pos:
        in_specs.append(pl.BlockSpec((tm, d), lambda b, j: (j, 0)))
        args.append(pos)
    consts = [g1, w1, b1, cs_hi, cs_lo, wg, bg, cum_f, cum_b]
    in_specs += [pl.BlockSpec((1, 6, d), (lambda b, j: (b, 0, 0)) if mod_b > 1 else (lambda b, j: (0, 0, 0)))]
    in_specs += [_resident(a.shape) for a in consts]
    args += [mod] + consts
    out_shape = [jax.ShapeDtypeStruct((bsz, t, 6 * GLA_K), BF16),
                 jax.ShapeDtypeStruct((bsz, t, GLA_V), BF16),
                 jax.ShapeDtypeStruct((bsz, t // CHUNK, 2 * GLA_K), F32),
                 jax.ShapeDtypeStruct((bsz, t, FNET_W), F32),
                 jax.ShapeDtypeStruct((bsz, t, FNET_W), F32),
                 jax.ShapeDtypeStruct((bsz, t, CONV_DIM), F32)]
    out_specs = [tok(6 * GLA_K), tok(GLA_V),
                 pl.BlockSpec((1, tm // CHUNK, 2 * GLA_K), lambda b, j: (b, j, 0)),
                 tok(FNET_W), tok(FNET_W), tok(CONV_DIM)]
    if add_pos:
        out_shape.append(jax.ShapeDtypeStruct((bsz, t, d), F32))
        out_specs.append(tok(d))
    return pl.pallas_call(
        functools.partial(_inproj_kernel, add_pos=add_pos),
        out_shape=out_shape, grid=(bsz, t // tm), in_specs=in_specs, out_specs=out_specs,
        compiler_params=_cparams(("parallel", "parallel")),
        name="inproj",
    )(*args)


def _gla_kernel(*refs, reverse, has_s0, add_ob, emit_state):
    refs = list(refs)
    gq_ref, v_ref, dec_ref = refs[:3]
    pos = 3
    s0_ref = ob_ref = sfin_ref = None
    if has_s0:
        s0_ref = refs[pos]; pos += 1
    if add_ob:
        ob_ref = refs[pos]; pos += 1
    o_ref = refs[pos]; pos += 1
    if emit_state:
        sfin_ref = refs[pos]; pos += 1
    st_ref = refs[pos]

    j = pl.program_id(1)

    @pl.when(j == 0)
    def _():
        for hh in range(N_HEADS):
            if has_s0:
                st_ref[hh] = s0_ref[0, hh].T
            else:
                st_ref[hh] = jnp.zeros((DV_HEAD, DK_HEAD), F32)

    r = v_ref.shape[1]
    sub = min(r, GLA_SUB)
    per_sub = sub // CHUNK
    row = lax.broadcasted_iota(jnp.int32, (sub, sub), 0)
    col = lax.broadcasted_iota(jnp.int32, (sub, sub), 1)
    first = row - (row & (CHUNK - 1))
    if reverse:
        keep = (col > row) & (col < first + CHUNK)
    else:
        keep = (col <= row) & (col >= first)
    flip = (lambda n: range(n - 1, -1, -1)) if reverse else range
    n_sub = r // sub
    blocks = [(hh, sb) for hh in range(N_HEADS) for sb in range(n_sub)]

    def operand(hh, sb, which):
        c0 = which * GLA_K + hh * DK_HEAD
        return gq_ref[0, sb * sub:(sb + 1) * sub, c0:c0 + DK_HEAD]

    s_raw, ds = {}, {}
    for hh, sb in blocks:
        s_raw[hh, sb] = _dot_nt(operand(hh, sb, 0), operand(hh, sb, 1))
        k_end = operand(hh, sb, 2)
        v = v_ref[0, sb * sub:(sb + 1) * sub, hh * DV_HEAD:(hh + 1) * DV_HEAD]
        for c in range(per_sub):
            cr = slice(c * CHUNK, (c + 1) * CHUNK)
            ds[hh, sb, c] = _dot_tn(v[cr], k_end[cr])
    st_in = {}
    for hh in range(N_HEADS):
        cs = slice(hh * DK_HEAD, (hh + 1) * DK_HEAD)
        st = st_ref[hh]
        for sb in flip(n_sub):
            for c in flip(per_sub):
                ci = sb * per_sub + c
                st_in[hh, sb, c] = st.astype(BF16)
                st = st * dec_ref[0, ci:ci + 1, cs] + ds[hh, sb, c]
        st_ref[hh] = st
    for hh, sb in blocks:
        rows = slice(sb * sub, (sb + 1) * sub)
        cs = slice(hh * DV_HEAD, (hh + 1) * DV_HEAD)
        q_dec = operand(hh, sb, 0)
        s = jnp.where(keep, s_raw[hh, sb], 0.0).astype(BF16)
        o_intra = _dot(s, v_ref[0, rows, cs])
        parts = []
        for c in range(per_sub):
            cr = slice(c * CHUNK, (c + 1) * CHUNK)
            parts.append(o_intra[cr] + _dot_nt(q_dec[cr], st_in[hh, sb, c]))
        o = jnp.concatenate(parts, axis=0)
        if add_ob:
            o = o + ob_ref[0, rows, cs]
        o_ref[0, rows, cs] = o

    if emit_state:
        @pl.when(j == pl.num_programs(1) - 1)
        def _():
            for hh in range(N_HEADS):
                sfin_ref[0, hh] = st_ref[hh].T


def _gla(gq, v, dec, s0, ob, *, reverse, emit_state):
    bsz, t, _ = v.shape
    tc = min(t, GLA_TILE)
    nblk = t // tc
    dr = 1 if reverse else 0
    blk = (lambda b, j: (b, nblk - 1 - j, 0)) if reverse else (lambda b, j: (b, j, 0))
    blk_dir = (lambda b, j: (b, nblk - 1 - j, dr)) if reverse else (lambda b, j: (b, j, dr))
    in_specs = [pl.BlockSpec((1, tc, 3 * GLA_K), blk_dir), pl.BlockSpec((1, tc, GLA_V), blk),
                pl.BlockSpec((1, tc // CHUNK, GLA_K), blk_dir)]
    args = [gq, v, dec]
    state_spec = pl.BlockSpec((1, N_HEADS, DK_HEAD, DV_HEAD), lambda b, j: (b, 0, 0, 0))
    if s0 is not None:
        in_specs.append(state_spec)
        args.append(s0)
    if ob is not None:
        in_specs.append(pl.BlockSpec((1, tc, GLA_V), blk))
        args.append(ob)
    out_shape = [jax.ShapeDtypeStruct((bsz, t, GLA_V), F32)]
    out_specs = [pl.BlockSpec((1, tc, GLA_V), blk)]
    if emit_state:
        out_shape.append(jax.ShapeDtypeStruct((bsz, N_HEADS, DK_HEAD, DV_HEAD), F32))
        out_specs.append(state_spec)
    res = pl.pallas_call(
        functools.partial(_gla_kernel, reverse=reverse, has_s0=s0 is not None,
                          add_ob=ob is not None, emit_state=emit_state),
        out_shape=out_shape, grid=(bsz, nblk), in_specs=in_specs, out_specs=out_specs,
        scratch_shapes=[pltpu.VMEM((N_HEADS, DV_HEAD, DK_HEAD), F32)],
        compiler_params=_cparams(("parallel", "arbitrary")),
        name="gla_bwd" if reverse else "gla_fwd",
    )(*args)
    return res if emit_state else (res[0], None)


def _dft_kernel(a_ref, ra_ref, b_ref, rb_ref, mid_ref, t1c_ref, t1s_ref, t2c_ref, t2s_ref, alt_ref,
                ylo_ref, z_ref, cm_ref, sm_ref, *, scale, sign_h):
    i = pl.program_id(0)
    tk = cm_ref.shape[0]
    n1 = tk // DFT_K2

    @pl.when(pl.program_id(1) == 0)
    def _():
        def gen(kk, carry):
            c1 = t1c_ref[pl.ds(i * n1 + kk, 1), :]
            s1 = t1s_ref[pl.ds(i * n1 + kk, 1), :]
            c2 = t2c_ref[...]
            s2 = t2s_ref[...]
            r0 = pl.multiple_of(kk * DFT_K2, DFT_K2)
            cm_ref[pl.ds(r0, DFT_K2), :] = (c1 * c2 - s1 * s2).astype(BF16)
            sm_ref[pl.ds(r0, DFT_K2), :] = (s1 * c2 + c1 * s2).astype(BF16)
            return carry
        lax.fori_loop(0, n1, gen, 0)

    ae = (a_ref[0] + ra_ref[0]).astype(BF16)
    bo = (b_ref[0] - rb_ref[0]).astype(BF16)
    p = _dot(cm_ref[...], ae)
    q = _dot(sm_ref[...], bo)
    mid = mid_ref[0, 0:1, :] * scale
    kidx = i * tk + lax.broadcasted_iota(jnp.int32, p.shape, 0)
    smid = jnp.where((kidx & 1) == 0, mid, -mid)
    ylo_ref[0] = (p - q + smid).astype(BF16)
    z = p + q + smid
    z_ref[0] = z.astype(BF16)

    @pl.when(i == 0)
    def _():
        nyq = _dot(alt_ref[...], ae)[0:1] * scale + sign_h * mid
        top = 2 * SUBLANES
        rows = lax.broadcasted_iota(jnp.int32, (top, z.shape[1]), 0)
        z_ref[0, 0:top, :] = jnp.where(rows == 0, nyq, z[0:top]).astype(BF16)


def _seq_dft(fa, fb, tabs):
    bsz, t, w = fa.shape
    h = t // 2
    tk = min(h, DFT_TILE)
    t1c, t1s, t2c, t2s, alt = tabs
    mirror = lambda a: jnp.roll(jnp.flip(a, axis=1), 1, axis=1)[:, :h]
    half = pl.BlockSpec((1, h, w), lambda i, b: (b, 0, 0))
    full = lambda a: pl.BlockSpec(a.shape, lambda i, b: (0, 0))
    out = pl.BlockSpec((1, tk, w), lambda i, b: (b, i, 0))
    ylo, z = pl.pallas_call(
        functools.partial(_dft_kernel, scale=1.0 / np.sqrt(float(t) * FNET_GROUP_DIM),
                          sign_h=1.0 if h % 2 == 0 else -1.0),
        out_shape=[jax.ShapeDtypeStruct((bsz, h, w), BF16)] * 2,
        grid=(h // tk, bsz),
        in_specs=[half, half, half, half,
                  pl.BlockSpec((1, SUBLANES, w), lambda i, b: (b, h // SUBLANES, 0)),
                  full(t1c), full(t1s), full(t2c), full(t2s), full(alt)],
        out_specs=[out, out],
        scratch_shapes=[pltpu.VMEM((tk, h), BF16), pltpu.VMEM((tk, h), BF16)],
        compiler_params=_cparams(("arbitrary", "arbitrary")),
        name="seq_dft",
    )(fa, mirror(fa), fb, mirror(fb), fa, t1c, t1s, t2c, t2s, alt)
    return jnp.concatenate([ylo, jnp.roll(jnp.flip(z, axis=1), 1, axis=1)], axis=1)


def _merge_kernel(x_ref, mod_ref, o_ref, yf_ref, u_ref, up_ref, un_ref,
                  g1_ref, w2_ref, b2_ref, ggla_ref, wa_ref, wb_ref,
                  cw_ref, cb_ref, lng_ref, lnb_ref, wc_ref, wo_ref,
                  out_ref):
    j = pl.program_id(1)
    tm = x_ref.shape[1]
    d = x_ref.shape[2]

    prev = jnp.where(j > 0, up_ref[0], 0.0)
    nxt = jnp.where(j < pl.num_programs(1) - 1, un_ref[0], 0.0)
    ext = jnp.concatenate([prev, u_ref[0], nxt], axis=0)
    n_ext = tm + 2 * CONV_HALO
    first_tap = CONV_HALO - (CONV_WIDTH - 1) // 2
    acc = jnp.zeros((tm, CONV_DIM), F32)
    for r in range(SUBLANES):
        rot = ext if r == 0 else pltpu.roll(ext, n_ext - r, 0)
        for a in range((first_tap + CONV_WIDTH - 1 - r) // SUBLANES + 1):
            k = SUBLANES * a + r - first_tap
            if 0 <= k < CONV_WIDTH:
                acc = acc + rot[SUBLANES * a:SUBLANES * a + tm] * cw_ref[k:k + 1, :]
    u = acc + cb_ref[...]
    mu = jnp.mean(u, axis=-1, keepdims=True)
    uc = u - mu
    u = uc * lax.rsqrt(jnp.mean(uc * uc, axis=-1, keepdims=True) + NORM_EPS) * lng_ref[...] + lnb_ref[...]
    u_act = _silu(u).astype(BF16)

    x = x_ref[0]
    h = (_rms(x, g1_ref[...]) * (1.0 + mod_ref[0, 1:2, :]) + mod_ref[0, 0:1, :]).astype(BF16)

    def gate(c0):
        return jax.nn.sigmoid(_dot(h, w2_ref[:, c0:c0 + d]) + b2_ref[:, c0:c0 + d])

    g = _dot(h, w2_ref[:, 0:GLA_V]) + b2_ref[:, 0:GLA_V]
    ggla = ggla_ref[...]
    o_gated = jnp.concatenate(
        [_rms(o_ref[0, :, hh * DV_HEAD:(hh + 1) * DV_HEAD], ggla) * _silu(g[:, hh * DV_HEAD:(hh + 1) * DV_HEAD])
         for hh in range(N_HEADS)], axis=1)
    merged = gate(GLA_V) * _dot(o_gated.astype(BF16), wa_ref[...])
    merged = merged + gate(GLA_V + d) * _dot(yf_ref[0], wb_ref[...])
    merged = merged + gate(GLA_V + 2 * d) * _dot(u_act, wc_ref[...])

    out_ref[0] = x + mod_ref[0, 2:3, :] * _dot(merged.astype(BF16), wo_ref[...])


def _merge(x, mod, o, yf, u0, g1, w2, b2, ggla, wa, wb, cw, cb, lng, lnb, wc, wo):
    bsz, t, d = x.shape
    tm = min(t, TOKEN_TILE)
    nt = t // tm
    hb = tm // CONV_HALO
    nh = t // CONV_HALO
    mod_b = mod.shape[0]
    tok = lambda w: pl.BlockSpec((1, tm, w), lambda b, j: (b, j, 0))
    in_specs = [tok(d),
                pl.BlockSpec((1, 6, d), (lambda b, j: (b, 0, 0)) if mod_b > 1 else (lambda b, j: (0, 0, 0))),
                tok(GLA_V), tok(FNET_W), tok(CONV_DIM),
                pl.BlockSpec((1, CONV_HALO, CONV_DIM), lambda b, j: (b, jnp.maximum(j * hb - 1, 0), 0)),
                pl.BlockSpec((1, CONV_HALO, CONV_DIM), lambda b, j: (b, jnp.minimum((j + 1) * hb, nh - 1), 0))]
    consts = [g1, w2, b2, ggla, wa, wb, cw, cb, lng, lnb, wc, wo]
    in_specs += [_resident(a.shape) for a in consts]
    return pl.pallas_call(
        _merge_kernel,
        out_shape=jax.ShapeDtypeStruct((bsz, t, d), F32),
        grid=(bsz, nt), in_specs=in_specs, out_specs=tok(d),
        compiler_params=_cparams(("parallel", "parallel")),
        name="merge",
    )(x, mod, o, yf, u0, u0, u0, *consts)


def _ffn_kernel(x_ref, xp_ref, xn_ref, mod_ref, g2_ref, wu_ref, fw_ref, fb_ref, wd_ref, gf_ref,
                out_ref, he_ref, act_ref, *, final):
    j = pl.program_id(1)
    tm = x_ref.shape[1]
    d_ff = wd_ref.shape[0]
    g2 = g2_ref[...]
    sc = 1.0 + mod_ref[0, 4:5, :]
    sh = mod_ref[0, 3:4, :]
    x = x_ref[0]
    hp = jnp.where(j > 0, _rms(xp_ref[0], g2) * sc + sh, 0.0)
    hn = jnp.where(j < pl.num_programs(1) - 1, _rms(xn_ref[0], g2) * sc + sh, 0.0)
    he_ref[0:FFN_HALO, :] = hp
    he_ref[FFN_HALO:FFN_HALO + tm, :] = _rms(x, g2) * sc + sh
    he_ref[FFN_HALO + tm:2 * FFN_HALO + tm, :] = hn
    he = he_ref[...].astype(BF16)
    n_ext = tm + 2 * FFN_HALO

    def conv3(u, c0):
        w = fw_ref[:, c0:c0 + FFN_COLS]
        um = pltpu.roll(u, 1, 0)[FFN_HALO:FFN_HALO + tm]
        up = pltpu.roll(u, n_ext - 1, 0)[FFN_HALO:FFN_HALO + tm]
        return (um * w[0:1] + u[FFN_HALO:FFN_HALO + tm] * w[1:2] + up * w[2:3]
                + fb_ref[:, c0:c0 + FFN_COLS])

    for c in range(d_ff // FFN_COLS):
        c0 = c * FFN_COLS
        gate = conv3(_dot(he, wu_ref[:, c0:c0 + FFN_COLS]), c0)
        val = conv3(_dot(he, wu_ref[:, d_ff + c0:d_ff + c0 + FFN_COLS]), d_ff + c0)
        act_ref[:, c0:c0 + FFN_COLS] = (_silu(gate) * val).astype(BF16)
    y = x + mod_ref[0, 5:6, :] * _dot(act_ref[...], wd_ref[...])
    if final:
        y = _rms(y, gf_ref[...])
    out_ref[0] = y


def _ffn(x, mod, g2, wu, fw, fb, wd, gf, *, final):
    bsz, t, d = x.shape
    tm = min(t, TOKEN_TILE)
    nt = t // tm
    hb = tm // FFN_HALO
    nh = t // FFN_HALO
    d_ff = wd.shape[0]
    mod_b = mod.shape[0]
    tok = pl.BlockSpec((1, tm, d), lambda b, j: (b, j, 0))
    in_specs = [tok,
                pl.BlockSpec((1, FFN_HALO, d), lambda b, j: (b, jnp.maximum(j * hb - 1, 0), 0)),
                pl.BlockSpec((1, FFN_HALO, d), lambda b, j: (b, jnp.minimum((j + 1) * hb, nh - 1), 0)),
                pl.BlockSpec((1, 6, d), (lambda b, j: (b, 0, 0)) if mod_b > 1 else (lambda b, j: (0, 0, 0)))]
    consts = [g2, wu, fw, fb, wd, gf]
    in_specs += [_resident(a.shape) for a in consts]
    return pl.pallas_call(
        functools.partial(_ffn_kernel, final=final),
        out_shape=jax.ShapeDtypeStruct((bsz, t, d), F32),
        grid=(bsz, nt), in_specs=in_specs, out_specs=tok,
        scratch_shapes=[pltpu.VMEM((tm + 2 * FFN_HALO, d), F32),
                        pltpu.VMEM((tm, d_ff), BF16)],
        compiler_params=_cparams(("parallel", "parallel")),
        name="convffn",
    )(x, x, x, mod, *consts)


def _pos_embed_2d(t, d):
    rows = t // GRID_W
    r = jnp.repeat(jnp.arange(rows, dtype=F32), GRID_W)
    col = jnp.tile(jnp.arange(GRID_W, dtype=F32), rows)
    nf = d // 4
    omega = 1.0 / (POS_BASE ** (jnp.arange(nf, dtype=F32) / nf))
    ar = r[:, None] * omega
    ac = col[:, None] * omega
    return jnp.concatenate([jnp.sin(ar), jnp.cos(ar), jnp.sin(ac), jnp.cos(ac)], axis=-1)


def _channel_dft_mats():
    idx = np.arange(FNET_GROUP_DIM)
    ang = 2.0 * np.pi * ((idx[:, None] * idx[None, :]) % FNET_GROUP_DIM) / FNET_GROUP_DIM
    eye = np.eye(FNET_GROUPS)
    cs = np.concatenate([np.kron(eye, np.cos(ang)), np.kron(eye, np.sin(ang))], axis=1).astype(np.float32)
    cs = jnp.asarray(cs)
    hi = cs.astype(BF16)
    lo = (cs - hi.astype(F32)).astype(BF16)
    return hi, lo


def _seq_dft_tables(t):
    h = t // 2
    scale = 1.0 / np.sqrt(float(t) * FNET_GROUP_DIM)
    tt = np.arange(h)
    ang = lambda k: 2.0 * np.pi * ((k[:, None] * tt[None, :]) % t) / t
    a1 = ang(np.arange(h // DFT_K2) * DFT_K2)
    a2 = ang(np.arange(DFT_K2))
    t2c = np.cos(a2)
    t2c[:, 0] = 0.5
    alt = np.zeros((2 * SUBLANES, h))
    alt[0] = 1.0 - 2.0 * (tt % 2)
    alt[0, 0] = 0.5
    tabs = [jnp.asarray(a, F32) for a in (np.cos(a1) * scale, np.sin(a1) * scale, t2c, np.sin(a2))]
    return tuple(tabs) + (jnp.asarray(alt, BF16),)


def kernel(x_prompt, x_sample, state_gla, c, c_ctx, w_ada, b_ada, g_norm1, w_in, b_in, w_gate2, b_gate2, g_gla, w_br_gla, w_br_fft, conv_w, conv_b, conv_ln_g, conv_ln_b, w_br_conv, w_out, g_norm2, w_up, ffn_conv_w, ffn_conv_b, w_down, g_final):
    depth, d, _ = w_in.shape
    n_ctx = x_prompt.shape[0]
    n_lat = x_sample.shape[0]
    t_lat = x_sample.shape[1]

    rows = -(-(1 + n_lat) // SUBLANES) * SUBLANES
    cvec = jnp.concatenate([c_ctx[None], c, jnp.zeros((rows - 1 - n_lat, d), F32)], axis=0)
    mod_all = _modulation(cvec, w_ada, b_ada).reshape(depth, rows, 6, d)

    sp = np.cumsum([0, GLA_K, GLA_K, GLA_V, GLA_V, 2 * GATE_RANK, FNET_W, 2 * CONV_DIM, N_BRANCH * d])
    q0, k0, v0, g0, lr0, uf0, uc0, mg0, end = (int(s) for s in sp)

    def cols1(a):
        lr_pad = jnp.zeros(a.shape[:-1] + (LR_PAD - 2 * GATE_RANK,), a.dtype)
        return jnp.concatenate([a[..., q0:g0], a[..., lr0:uf0], lr_pad, a[..., uf0:mg0]], axis=-1)

    def cols2(a):
        return jnp.concatenate([a[..., g0:lr0], a[..., mg0:end]], axis=-1)

    ridx = np.arange(GLA_SUB)
    same = (ridx[:, None] // CHUNK) == (ridx[None, :] // CHUNK)
    cum_f = jnp.asarray(same & (ridx[None, :] <= ridx[:, None]), BF16)
    cum_b = jnp.asarray(same & (ridx[None, :] >= ridx[:, None]), BF16)
    cs_hi, cs_lo = _channel_dft_mats()
    dft = {t: _seq_dft_tables(t) for t in {x_prompt.shape[1], t_lat}}
    pos = _pos_embed_2d(t_lat, d)
    row = lambda a: a[None]

    def layer_weights(l):
        wg = jnp.zeros((LR_PAD, 2 * GLA_K), F32)
        wg = wg.at[0:GATE_RANK, 0:GLA_K].set(w_gate2[l, 0])
        wg = wg.at[GATE_RANK:2 * GATE_RANK, GLA_K:].set(w_gate2[l, 1]).astype(BF16)
        return dict(
            g1=row(g_norm1[l]), w1=cols1(w_in[l]).astype(BF16), b1=row(cols1(b_in[l])),
            wg=wg, bg=b_gate2[l].reshape(1, 2 * GLA_K),
            w2=cols2(w_in[l]).astype(BF16), b2=row(cols2(b_in[l])), ggla=row(g_gla[l]),
            wa=w_br_gla[l].astype(BF16), wb=w_br_fft[l].astype(BF16), cw=conv_w[l], cb=row(conv_b[l]),
            lng=row(conv_ln_g[l]), lnb=row(conv_ln_b[l]), wc=w_br_conv[l].astype(BF16), wo=w_out[l].astype(BF16),
            g2=row(g_norm2[l]), wu=w_up[l].astype(BF16), fw=ffn_conv_w[l], fb=row(ffn_conv_b[l]),
            wd=w_down[l].astype(BF16))

    def layer(x, pos_in, mod, s0, p, emit_state, final):
        res = _inproj(x, pos_in, mod, p["g1"], p["w1"], p["b1"], cs_hi, cs_lo, p["wg"], p["bg"], cum_f, cum_b)
        gq, v, dec, fa, fb, u0 = res[:6]
        if pos_in is not None:
            x = res[6]
        s0_f = None if s0 is None else s0[:, 0]
        s0_b = None if s0 is None else s0[:, 1]
        ob, s_b = _gla(gq, v, dec, s0_b, None, reverse=True, emit_state=emit_state)
        o, s_f = _gla(gq, v, dec, s0_f, ob, reverse=False, emit_state=emit_state)
        yf = _seq_dft(fa, fb, dft[x.shape[1]])
        x = _merge(x, mod, o, yf, u0, p["g1"], p["w2"], p["b2"], p["ggla"], p["wa"], p["wb"],
                   p["cw"], p["cb"], p["lng"], p["lnb"], p["wc"], p["wo"])
        x = _ffn(x, mod, p["g2"], p["wu"], p["fw"], p["fb"], p["wd"], row(g_final), final=final)
        return x, s_f, s_b

    xp, xs = x_prompt, x_sample
    new_states = []
    for l in range(depth):
        final = l == depth - 1
        p = layer_weights(l)
        xp, s_f, s_b = layer(xp, None, mod_all[l, 0:1], None, p, True, final)
        new_states.append(jnp.stack([s_f, s_b], axis=1))
        xs, _, _ = layer(xs, pos if l == 0 else None, mod_all[l, 1:1 + n_lat], state_gla[:, l], p, False, final)
    return xp, xs, jnp.stack(new_states, axis=1)
```

```python
import functools

import numpy as np
import jax
import jax.numpy as jnp
from jax import lax
from jax.experimental import pallas as pl
from jax.experimental.pallas import tpu as pltpu

F32 = jnp.float32
BF16 = jnp.bfloat16

GRID_W = 64
N_HEADS = 4
DK_HEAD = 128
DV_HEAD = 128
GLA_K = N_HEADS * DK_HEAD
GLA_V = N_HEADS * DV_HEAD
GATE_RANK = 16
GATE_TAU = 16.0
CHUNK = 64
FNET_GROUPS = 4
FNET_GROUP_DIM = 64
FNET_W = FNET_GROUPS * FNET_GROUP_DIM
CONV_DIM = 256
CONV_WIDTH = 31
N_BRANCH = 3
FFN_CONV_WIDTH = 3
NORM_EPS = 1e-6
POS_BASE = 10000.0

LANES = 128
SUBLANES = 8
LR_PAD = LANES
CONV_HALO = 16
FFN_HALO = SUBLANES
TOKEN_TILE = 512
GLA_TILE = 512
GLA_SUB = 256
DFT_REV_BLOCK = 256
DFT_K2 = 32
FFN_COLS = 256
VMEM_LIMIT = 56 * 1024 * 1024


def _cparams(sem):
    return pltpu.CompilerParams(dimension_semantics=sem, vmem_limit_bytes=VMEM_LIMIT)


def _resident(shape):
    nd = len(shape)
    return pl.BlockSpec(shape, lambda *_: (0,) * nd, pipeline_mode=pl.Buffered(1))


def _dot(a, b):
    return jnp.dot(a, b, preferred_element_type=F32)


def _dot_nt(a, b):
    return lax.dot_general(a, b, (((1,), (1,)), ((), ())), preferred_element_type=F32)


def _dot_tn(a, b):
    return lax.dot_general(a, b, (((0,), (0,)), ((), ())), preferred_element_type=F32)


def _silu(x):
    return x * jax.nn.sigmoid(x)


def _rms(x, g):
    return x * lax.rsqrt(jnp.mean(x * x, axis=-1, keepdims=True) + NORM_EPS) * g


def _split2(x):
    hi = x.astype(BF16)
    return hi, (x - hi.astype(F32)).astype(BF16)


def _mod_kernel(c_ref, w_ref, b_ref, o_ref):
    act = _silu(c_ref[...]).astype(BF16)
    o_ref[0] = _dot(act, w_ref[0].astype(BF16)) + b_ref[0]


def _modulation(cvec, w_ada, b_ada):
    depth, d, n = w_ada.shape
    rows = cvec.shape[0]
    tn = 1024
    return pl.pallas_call(
        _mod_kernel,
        out_shape=jax.ShapeDtypeStruct((depth, rows, n), F32),
        grid=(depth, n // tn),
        in_specs=[pl.BlockSpec((rows, d), lambda l, j: (0, 0)),
                  pl.BlockSpec((1, d, tn), lambda l, j: (l, 0, j)),
                  pl.BlockSpec((1, 1, tn), lambda l, j: (l, 0, j))],
        out_specs=pl.BlockSpec((1, rows, tn), lambda l, j: (l, 0, j)),
        compiler_params=_cparams(("arbitrary", "arbitrary")),
        name="modulation",
    )(cvec, w_ada, b_ada.reshape(depth, 1, n))


def _inproj_kernel(*refs, add_pos):
    if add_pos:
        (x_ref, pos_ref, mod_ref, g1_ref, w_ref, b_ref, csh_ref, csl_ref, wg_ref, bg_ref, cumf_ref, cumb_ref,
         gq_ref, v_ref, dec_ref, fa_ref, fb_ref, u0_ref, xs_ref) = refs
        x = x_ref[0] + pos_ref[...]
        xs_ref[0] = x
    else:
        (x_ref, mod_ref, g1_ref, w_ref, b_ref, csh_ref, csl_ref, wg_ref, bg_ref, cumf_ref, cumb_ref,
         gq_ref, v_ref, dec_ref, fa_ref, fb_ref, u0_ref) = refs
        x = x_ref[0]
    tm = x.shape[0]
    h = (_rms(x, g1_ref[...]) * (1.0 + mod_ref[0, 1:2, :]) + mod_ref[0, 0:1, :]).astype(BF16)

    def proj(c0, width):
        return _dot(h, w_ref[:, c0:c0 + width]) + b_ref[:, c0:c0 + width]

    q0, k0, v0, lr0 = 0, GLA_K, 2 * GLA_K, 2 * GLA_K + GLA_V
    uf0 = lr0 + LR_PAD
    uc0 = uf0 + FNET_W
    lr = proj(lr0, LR_PAD).astype(BF16)
    logit = _dot(lr, wg_ref[...]) + bg_ref[...]
    log_a = (jnp.minimum(logit, 0.0) - jnp.log(1.0 + jnp.exp(-jnp.abs(logit)))) * (1.0 / GATE_TAU)
    la_hi, la_lo = _split2(log_a)
    zq = proj(q0, GLA_K) * (DK_HEAD ** -0.5)
    zk = proj(k0, GLA_K)
    v_ref[0] = proj(v0, GLA_V).astype(BF16)
    uf_hi, uf_lo = _split2(proj(uf0, FNET_W))
    ab = _dot(uf_hi, csh_ref[...]) + _dot(uf_lo, csh_ref[...]) + _dot(uf_hi, csl_ref[...])
    fa_ref[0] = ab[:, :FNET_W].astype(BF16)
    fb_ref[0] = ab[:, FNET_W:].astype(BF16)
    uc = proj(uc0, 2 * CONV_DIM)
    u0_ref[0] = uc[:, :CONV_DIM] * jax.nn.sigmoid(uc[:, CONV_DIM:])
    per_sub = GLA_SUB // CHUNK
    for sb in range(tm // GLA_SUB):
        rows = slice(sb * GLA_SUB, (sb + 1) * GLA_SUB)
        for dr in range(2):
            cols = slice(dr * GLA_K, (dr + 1) * GLA_K)
            cum = cumb_ref[...] if dr else cumf_ref[...]
            b = _dot(cum, la_hi[rows, cols]) + _dot(cum, la_lo[rows, cols])
            b_last = [b[c * CHUNK:c * CHUNK + 1] if dr else b[(c + 1) * CHUNK - 1:(c + 1) * CHUNK]
                      for c in range(per_sub)]
            b_last_rows = jnp.concatenate([jnp.broadcast_to(bl, (CHUNK, GLA_K)) for bl in b_last], axis=0)
            g0 = dr * 3 * GLA_K
            gq_ref[0, rows, g0:g0 + GLA_K] = (zq[rows] * jnp.exp(b)).astype(BF16)
            gq_ref[0, rows, g0 + GLA_K:g0 + 2 * GLA_K] = (zk[rows] * jnp.exp(-b)).astype(BF16)
            gq_ref[0, rows, g0 + 2 * GLA_K:g0 + 3 * GLA_K] = (zk[rows] * jnp.exp(b_last_rows - b)).astype(BF16)
            for c in range(per_sub):
                dec_ref[0, sb * per_sub + c:sb * per_sub + c + 1, cols] = jnp.exp(b_last[c])


def _inproj(x, pos, mod, g1, w1, b1, cs_hi, cs_lo, wg, bg, cum_f, cum_b):
    bsz, t, d = x.shape
    tm = min(t, TOKEN_TILE)
    add_pos = pos is not None
    mod_b = mod.shape[0]
    tok = lambda w: pl.BlockSpec((1, tm, w), lambda b, j: (b, j, 0))
    in_specs = [tok(d)]
    args = [x]
    if add_pos:
        in_specs.append(pl.BlockSpec((tm, d), lambda b, j: (j, 0)))
        args.append(pos)
    consts = [g1, w1, b1, cs_hi, cs_lo, wg, bg, cum_f, cum_b]
    in_specs += [pl.BlockSpec((1, 6, d), (lambda b, j: (b, 0, 0)) if mod_b > 1 else (lambda b, j: (0, 0, 0)))]
    in_specs += [_resident(a.shape) for a in consts]
    args += [mod] + consts
    out_shape = [jax.ShapeDtypeStruct((bsz, t, 6 * GLA_K), BF16),
                 jax.ShapeDtypeStruct((bsz, t, GLA_V), BF16),
                 jax.ShapeDtypeStruct((bsz, t // CHUNK, 2 * GLA_K), F32),
                 jax.ShapeDtypeStruct((bsz, t, FNET_W), BF16),
                 jax.ShapeDtypeStruct((bsz, t, FNET_W), BF16),
                 jax.ShapeDtypeStruct((bsz, t, CONV_DIM), F32)]
    out_specs = [tok(6 * GLA_K), tok(GLA_V),
                 pl.BlockSpec((1, tm // CHUNK, 2 * GLA_K), lambda b, j: (b, j, 0)),
                 tok(FNET_W), tok(FNET_W), tok(CONV_DIM)]
    if add_pos:
        out_shape.append(jax.ShapeDtypeStruct((bsz, t, d), F32))
        out_specs.append(tok(d))
    return pl.pallas_call(
        functools.partial(_inproj_kernel, add_pos=add_pos),
        out_shape=out_shape, grid=(bsz, t // tm), in_specs=in_specs, out_specs=out_specs,
        compiler_params=_cparams(("parallel", "parallel")),
        name="inproj",
    )(*args)


def _gla_kernel(*refs, reverse, has_s0, add_ob, emit_state):
    refs = list(refs)
    gq_ref, v_ref, dec_ref = refs[:3]
    pos = 3
    s0_ref = ob_ref = sfin_ref = None
    if has_s0:
        s0_ref = refs[pos]; pos += 1
    if add_ob:
        ob_ref = refs[pos]; pos += 1
    o_ref = refs[pos]; pos += 1
    if emit_state:
        sfin_ref = refs[pos]; pos += 1
    st_ref = refs[pos]

    j = pl.program_id(1)

    @pl.when(j == 0)
    def _():
        for hh in range(N_HEADS):
            if has_s0:
                st_ref[hh] = s0_ref[0, hh].T
            else:
                st_ref[hh] = jnp.zeros((DV_HEAD, DK_HEAD), F32)

    r = v_ref.shape[1]
    sub = min(r, GLA_SUB)
    per_sub = sub // CHUNK
    row = lax.broadcasted_iota(jnp.int32, (sub, sub), 0)
    col = lax.broadcasted_iota(jnp.int32, (sub, sub), 1)
    first = row - (row & (CHUNK - 1))
    if reverse:
        keep = (col > row) & (col < first + CHUNK)
    else:
        keep = (col <= row) & (col >= first)
    flip = (lambda n: range(n - 1, -1, -1)) if reverse else range
    n_sub = r // sub
    blocks = [(hh, sb) for hh in range(N_HEADS) for sb in range(n_sub)]

    def operand(hh, sb, which):
        c0 = which * GLA_K + hh * DK_HEAD
        return gq_ref[0, sb * sub:(sb + 1) * sub, c0:c0 + DK_HEAD]

    s_raw, ds = {}, {}
    for hh, sb in blocks:
        s_raw[hh, sb] = _dot_nt(operand(hh, sb, 0), operand(hh, sb, 1))
        k_end = operand(hh, sb, 2)
        v = v_ref[0, sb * sub:(sb + 1) * sub, hh * DV_HEAD:(hh + 1) * DV_HEAD]
        for c in range(per_sub):
            cr = slice(c * CHUNK, (c + 1) * CHUNK)
            ds[hh, sb, c] = _dot_tn(v[cr], k_end[cr])
    st_in = {}
    for hh in range(N_HEADS):
        cs = slice(hh * DK_HEAD, (hh + 1) * DK_HEAD)
        st = st_ref[hh]
        for sb in flip(n_sub):
            for c in flip(per_sub):
                ci = sb * per_sub + c
                st_in[hh, sb, c] = st.astype(BF16)
                st = st * dec_ref[0, ci:ci + 1, cs] + ds[hh, sb, c]
        st_ref[hh] = st
    for hh, sb in blocks:
        rows = slice(sb * sub, (sb + 1) * sub)
        cs = slice(hh * DV_HEAD, (hh + 1) * DV_HEAD)
        q_dec = operand(hh, sb, 0)
        s = jnp.where(keep, s_raw[hh, sb], 0.0).astype(BF16)
        o_intra = _dot(s, v_ref[0, rows, cs])
        parts = []
        for c in range(per_sub):
            cr = slice(c * CHUNK, (c + 1) * CHUNK)
            parts.append(o_intra[cr] + _dot_nt(q_dec[cr], st_in[hh, sb, c]))
        o = jnp.concatenate(parts, axis=0)
        if add_ob:
            o = o + ob_ref[0, rows, cs]
        o_ref[0, rows, cs] = o

    if emit_state:
        @pl.when(j == pl.num_programs(1) - 1)
        def _():
            for hh in range(N_HEADS):
                sfin_ref[0, hh] = st_ref[hh].T


def _gla(gq, v, dec, s0, ob, *, reverse, emit_state):
    bsz, t, _ = v.shape
    tc = min(t, GLA_TILE)
    nblk = t // tc
    dr = 1 if reverse else 0
    blk = (lambda b, j: (b, nblk - 1 - j, 0)) if reverse else (lambda b, j: (b, j, 0))
    blk_dir = (lambda b, j: (b, nblk - 1 - j, dr)) if reverse else (lambda b, j: (b, j, dr))
    in_specs = [pl.BlockSpec((1, tc, 3 * GLA_K), blk_dir), pl.BlockSpec((1, tc, GLA_V), blk),
                pl.BlockSpec((1, tc // CHUNK, GLA_K), blk_dir)]
    args = [gq, v, dec]
    state_spec = pl.BlockSpec((1, N_HEADS, DK_HEAD, DV_HEAD), lambda b, j: (b, 0, 0, 0))
    if s0 is not None:
        in_specs.append(state_spec)
        args.append(s0)
    if ob is not None:
        in_specs.append(pl.BlockSpec((1, tc, GLA_V), blk))
        args.append(ob)
    out_shape = [jax.ShapeDtypeStruct((bsz, t, GLA_V), F32)]
    out_specs = [pl.BlockSpec((1, tc, GLA_V), blk)]
    if emit_state:
        out_shape.append(jax.ShapeDtypeStruct((bsz, N_HEADS, DK_HEAD, DV_HEAD), F32))
        out_specs.append(state_spec)
    res = pl.pallas_call(
        functools.partial(_gla_kernel, reverse=reverse, has_s0=s0 is not None,
                          add_ob=ob is not None, emit_state=emit_state),
        out_shape=out_shape, grid=(bsz, nblk), in_specs=in_specs, out_specs=out_specs,
        scratch_shapes=[pltpu.VMEM((N_HEADS, DV_HEAD, DK_HEAD), F32)],
        compiler_params=_cparams(("parallel", "arbitrary")),
        name="gla_bwd" if reverse else "gla_fwd",
    )(*args)
    return res if emit_state else (res[0], None)


def _reverse_rows(x, j_ref):
    blk = j_ref.shape[0]
    nb = x.shape[0] // blk
    return jnp.concatenate([_dot(j_ref[...], x[(nb - 1 - m) * blk:(nb - m) * blk]) for m in range(nb)], axis=0)


def _dft_kernel(a_ref, b_ref, t1c_ref, t1s_ref, t2c_ref, t2s_ref, alt_ref, j_ref,
                y_ref, cm_ref, sm_ref, *, scale, sign_h):
    h = cm_ref.shape[0]

    @pl.when(pl.program_id(0) == 0)
    def _():
        def gen(kk, carry):
            c1 = t1c_ref[pl.ds(kk, 1), :]
            s1 = t1s_ref[pl.ds(kk, 1), :]
            c2 = t2c_ref[...]
            s2 = t2s_ref[...]
            r0 = pl.multiple_of(kk * DFT_K2, DFT_K2)
            cm_ref[pl.ds(r0, DFT_K2), :] = (c1 * c2 - s1 * s2).astype(BF16)
            sm_ref[pl.ds(r0, DFT_K2), :] = (s1 * c2 + c1 * s2).astype(BF16)
            return carry
        lax.fori_loop(0, h // DFT_K2, gen, 0)

    row0 = lax.broadcasted_iota(jnp.int32, (h, a_ref.shape[2]), 0) == 0

    def mirrored(ref):
        rev = _reverse_rows(ref[0, h:2 * h, :], j_ref)
        return jnp.where(row0, ref[0, 0:1, :].astype(F32), pltpu.roll(rev, 1, 0))

    ae = (a_ref[0, 0:h, :].astype(F32) + mirrored(a_ref)).astype(BF16)
    bo = (b_ref[0, 0:h, :].astype(F32) - mirrored(b_ref)).astype(BF16)
    mid = a_ref[0, h:h + 1, :].astype(F32) * scale
    nyq = _dot(alt_ref[...], ae)[0:1] * scale + sign_h * mid
    p = _dot(cm_ref[...], ae)
    q = _dot(sm_ref[...], bo)
    kidx = lax.broadcasted_iota(jnp.int32, p.shape, 0)
    smid = jnp.where((kidx & 1) == 0, mid, -mid)
    y_ref[0, 0:h, :] = (p - q + smid).astype(BF16)
    z = jnp.where(row0, nyq, p + q + smid).astype(BF16)
    y_ref[0, h:2 * h, :] = pltpu.roll(_reverse_rows(z, j_ref), 1, 0).astype(BF16)


def _seq_dft(fa, fb, tabs):
    bsz, t, w = fa.shape
    h = t // 2
    seq = pl.BlockSpec((1, t, w), lambda b: (b, 0, 0))
    return pl.pallas_call(
        functools.partial(_dft_kernel, scale=1.0 / np.sqrt(float(t) * FNET_GROUP_DIM),
                          sign_h=1.0 if h % 2 == 0 else -1.0),
        out_shape=jax.ShapeDtypeStruct((bsz, t, w), BF16),
        grid=(bsz,),
        in_specs=[seq, seq] + [_resident(a.shape) for a in tabs],
        out_specs=seq,
        scratch_shapes=[pltpu.VMEM((h, h), BF16), pltpu.VMEM((h, h), BF16)],
        compiler_params=_cparams(("arbitrary",)),
        name="seq_dft",
    )(fa, fb, *tabs)


def _merge_kernel(x_ref, mod_ref, o_ref, yf_ref, u_ref, up_ref, un_ref,
                  g1_ref, w2_ref, b2_ref, ggla_ref, wa_ref, wb_ref,
                  cw_ref, cb_ref, lng_ref, lnb_ref, wc_ref, wo_ref,
                  out_ref):
    j = pl.program_id(1)
    tm = x_ref.shape[1]
    d = x_ref.shape[2]

    prev = jnp.where(j > 0, up_ref[0], 0.0)
    nxt = jnp.where(j < pl.num_programs(1) - 1, un_ref[0], 0.0)
    ext = jnp.concatenate([prev, u_ref[0], nxt], axis=0)
    n_ext = tm + 2 * CONV_HALO
    first_tap = CONV_HALO - (CONV_WIDTH - 1) // 2
    acc = jnp.zeros((tm, CONV_DIM), F32)
    for r in range(SUBLANES):
        rot = ext if r == 0 else pltpu.roll(ext, n_ext - r, 0)
        for a in range((first_tap + CONV_WIDTH - 1 - r) // SUBLANES + 1):
            k = SUBLANES * a + r - first_tap
            if 0 <= k < CONV_WIDTH:
                acc = acc + rot[SUBLANES * a:SUBLANES * a + tm] * cw_ref[k:k + 1, :]
    u = acc + cb_ref[...]
    mu = jnp.mean(u, axis=-1, keepdims=True)
    uc = u - mu
    u = uc * lax.rsqrt(jnp.mean(uc * uc, axis=-1, keepdims=True) + NORM_EPS) * lng_ref[...] + lnb_ref[...]
    u_act = _silu(u).astype(BF16)

    x = x_ref[0]
    h = (_rms(x, g1_ref[...]) * (1.0 + mod_ref[0, 1:2, :]) + mod_ref[0, 0:1, :]).astype(BF16)

    gates = [jax.nn.sigmoid(_dot(h, w2_ref[:, c0:c0 + d]) + b2_ref[:, c0:c0 + d])
             for c0 in (GLA_V, GLA_V + d, GLA_V + 2 * d)]
    g = _dot(h, w2_ref[:, 0:GLA_V]) + b2_ref[:, 0:GLA_V]
    ggla = ggla_ref[...]
    o_gated = jnp.concatenate(
        [_rms(o_ref[0, :, hh * DV_HEAD:(hh + 1) * DV_HEAD], ggla) * _silu(g[:, hh * DV_HEAD:(hh + 1) * DV_HEAD])
         for hh in range(N_HEADS)], axis=1)
    merged = gates[0] * _dot(o_gated.astype(BF16), wa_ref[...])
    merged = merged + gates[1] * _dot(yf_ref[0], wb_ref[...])
    merged = merged + gates[2] * _dot(u_act, wc_ref[...])

    out_ref[0] = x + mod_ref[0, 2:3, :] * _dot(merged.astype(BF16), wo_ref[...])


def _merge(x, mod, o, yf, u0, g1, w2, b2, ggla, wa, wb, cw, cb, lng, lnb, wc, wo):
    bsz, t, d = x.shape
    tm = min(t, TOKEN_TILE)
    nt = t // tm
    hb = tm // CONV_HALO
    nh = t // CONV_HALO
    mod_b = mod.shape[0]
    tok = lambda w: pl.BlockSpec((1, tm, w), lambda b, j: (b, j, 0))
    in_specs = [tok(d),
                pl.BlockSpec((1, 6, d), (lambda b, j: (b, 0, 0)) if mod_b > 1 else (lambda b, j: (0, 0, 0))),
                tok(GLA_V), tok(FNET_W), tok(CONV_DIM),
                pl.BlockSpec((1, CONV_HALO, CONV_DIM), lambda b, j: (b, jnp.maximum(j * hb - 1, 0), 0)),
                pl.BlockSpec((1, CONV_HALO, CONV_DIM), lambda b, j: (b, jnp.minimum((j + 1) * hb, nh - 1), 0))]
    consts = [g1, w2, b2, ggla, wa, wb, cw, cb, lng, lnb, wc, wo]
    in_specs += [_resident(a.shape) for a in consts]
    return pl.pallas_call(
        _merge_kernel,
        out_shape=jax.ShapeDtypeStruct((bsz, t, d), F32),
        grid=(bsz, nt), in_specs=in_specs, out_specs=tok(d),
        compiler_params=_cparams(("parallel", "parallel")),
        name="merge",
    )(x, mod, o, yf, u0, u0, u0, *consts)


def _ffn_kernel(x_ref, xp_ref, xn_ref, mod_ref, g2_ref, wu_ref, fw_ref, fb_ref, wd_ref, gf_ref,
                out_ref, he_ref, act_ref, *, final):
    j = pl.program_id(1)
    tm = x_ref.shape[1]
    d_ff = wd_ref.shape[0]
    g2 = g2_ref[...]
    sc = 1.0 + mod_ref[0, 4:5, :]
    sh = mod_ref[0, 3:4, :]
    x = x_ref[0]
    hp = jnp.where(j > 0, _rms(xp_ref[0], g2) * sc + sh, 0.0)
    hn = jnp.where(j < pl.num_programs(1) - 1, _rms(xn_ref[0], g2) * sc + sh, 0.0)
    he_ref[0:FFN_HALO, :] = hp
    he_ref[FFN_HALO:FFN_HALO + tm, :] = _rms(x, g2) * sc + sh
    he_ref[FFN_HALO + tm:2 * FFN_HALO + tm, :] = hn
    he = he_ref[...].astype(BF16)
    n_ext = tm + 2 * FFN_HALO

    def conv3(u, c0):
        w = fw_ref[:, c0:c0 + FFN_COLS]
        um = pltpu.roll(u, 1, 0)[FFN_HALO:FFN_HALO + tm]
        up = pltpu.roll(u, n_ext - 1, 0)[FFN_HALO:FFN_HALO + tm]
        return (um * w[0:1] + u[FFN_HALO:FFN_HALO + tm] * w[1:2] + up * w[2:3]
                + fb_ref[:, c0:c0 + FFN_COLS])

    for c in range(d_ff // FFN_COLS):
        c0 = c * FFN_COLS
        gate = conv3(_dot(he, wu_ref[:, c0:c0 + FFN_COLS]), c0)
        val = conv3(_dot(he, wu_ref[:, d_ff + c0:d_ff + c0 + FFN_COLS]), d_ff + c0)
        act_ref[:, c0:c0 + FFN_COLS] = (_silu(gate) * val).astype(BF16)
    y = x + mod_ref[0, 5:6, :] * _dot(act_ref[...], wd_ref[...])
    if final:
        y = _rms(y, gf_ref[...])
    out_ref[0] = y


def _ffn(x, mod, g2, wu, fw, fb, wd, gf, *, final):
    bsz, t, d = x.shape
    tm = min(t, TOKEN_TILE)
    nt = t // tm
    hb = tm // FFN_HALO
    nh = t // FFN_HALO
    d_ff = wd.shape[0]
    mod_b = mod.shape[0]
    tok = pl.BlockSpec((1, tm, d), lambda b, j: (b, j, 0))
    in_specs = [tok,
                pl.BlockSpec((1, FFN_HALO, d), lambda b, j: (b, jnp.maximum(j * hb - 1, 0), 0)),
                pl.BlockSpec((1, FFN_HALO, d), lambda b, j: (b, jnp.minimum((j + 1) * hb, nh - 1), 0)),
                pl.BlockSpec((1, 6, d), (lambda b, j: (b, 0, 0)) if mod_b > 1 else (lambda b, j: (0, 0, 0)))]
    consts = [g2, wu, fw, fb, wd, gf]
    in_specs += [_resident(a.shape) for a in consts]
    return pl.pallas_call(
        functools.partial(_ffn_kernel, final=final),
        out_shape=jax.ShapeDtypeStruct((bsz, t, d), F32),
        grid=(bsz, nt), in_specs=in_specs, out_specs=tok,
        scratch_shapes=[pltpu.VMEM((tm + 2 * FFN_HALO, d), F32),
                        pltpu.VMEM((tm, d_ff), BF16)],
        compiler_params=_cparams(("parallel", "parallel")),
        name="convffn",
    )(x, x, x, mod, *consts)


def _pos_embed_2d(t, d):
    rows = t // GRID_W
    r = jnp.repeat(jnp.arange(rows, dtype=F32), GRID_W)
    col = jnp.tile(jnp.arange(GRID_W, dtype=F32), rows)
    nf = d // 4
    omega = 1.0 / (POS_BASE ** (jnp.arange(nf, dtype=F32) / nf))
    ar = r[:, None] * omega
    ac = col[:, None] * omega
    return jnp.concatenate([jnp.sin(ar), jnp.cos(ar), jnp.sin(ac), jnp.cos(ac)], axis=-1)


def _channel_dft_mats():
    idx = np.arange(FNET_GROUP_DIM)
    ang = 2.0 * np.pi * ((idx[:, None] * idx[None, :]) % FNET_GROUP_DIM) / FNET_GROUP_DIM
    eye = np.eye(FNET_GROUPS)
    cs = np.concatenate([np.kron(eye, np.cos(ang)), np.kron(eye, np.sin(ang))], axis=1).astype(np.float32)
    cs = jnp.asarray(cs)
    hi = cs.astype(BF16)
    lo = (cs - hi.astype(F32)).astype(BF16)
    return hi, lo


def _seq_dft_tables(t):
    h = t // 2
    scale = 1.0 / np.sqrt(float(t) * FNET_GROUP_DIM)
    tt = np.arange(h)
    ang = lambda k: 2.0 * np.pi * ((k[:, None] * tt[None, :]) % t) / t
    a1 = ang(np.arange(h // DFT_K2) * DFT_K2)
    a2 = ang(np.arange(DFT_K2))
    t2c = np.cos(a2)
    t2c[:, 0] = 0.5
    alt = np.zeros((2 * SUBLANES, h))
    alt[0] = 1.0 - 2.0 * (tt % 2)
    alt[0, 0] = 0.5
    tabs = [jnp.asarray(a, F32) for a in (np.cos(a1) * scale, np.sin(a1) * scale, t2c, np.sin(a2))]
    anti = np.eye(min(h, DFT_REV_BLOCK))[::-1]
    return tuple(tabs) + (jnp.asarray(alt, BF16), jnp.asarray(anti, BF16))


def kernel(x_prompt, x_sample, state_gla, c, c_ctx, w_ada, b_ada, g_norm1, w_in, b_in, w_gate2, b_gate2, g_gla, w_br_gla, w_br_fft, conv_w, conv_b, conv_ln_g, conv_ln_b, w_br_conv, w_out, g_norm2, w_up, ffn_conv_w, ffn_conv_b, w_down, g_final):
    depth, d, _ = w_in.shape
    n_ctx = x_prompt.shape[0]
    n_lat = x_sample.shape[0]
    t_lat = x_sample.shape[1]

    rows = -(-(1 + n_lat) // SUBLANES) * SUBLANES
    cvec = jnp.concatenate([c_ctx[None], c, jnp.zeros((rows - 1 - n_lat, d), F32)], axis=0)
    mod_all = _modulation(cvec, w_ada, b_ada).reshape(depth, rows, 6, d)

    sp = np.cumsum([0, GLA_K, GLA_K, GLA_V, GLA_V, 2 * GATE_RANK, FNET_W, 2 * CONV_DIM, N_BRANCH * d])
    q0, k0, v0, g0, lr0, uf0, uc0, mg0, end = (int(s) for s in sp)

    def cols1(a):
        lr_pad = jnp.zeros(a.shape[:-1] + (LR_PAD - 2 * GATE_RANK,), a.dtype)
        return jnp.concatenate([a[..., q0:g0], a[..., lr0:uf0], lr_pad, a[..., uf0:mg0]], axis=-1)

    def cols2(a):
        return jnp.concatenate([a[..., g0:lr0], a[..., mg0:end]], axis=-1)

    ridx = np.arange(GLA_SUB)
    same = (ridx[:, None] // CHUNK) == (ridx[None, :] // CHUNK)
    cum_f = jnp.asarray(same & (ridx[None, :] <= ridx[:, None]), BF16)
    cum_b = jnp.asarray(same & (ridx[None, :] >= ridx[:, None]), BF16)
    cs_hi, cs_lo = _channel_dft_mats()
    dft = {t: _seq_dft_tables(t) for t in {x_prompt.shape[1], t_lat}}
    pos = _pos_embed_2d(t_lat, d)
    row = lambda a: a[None]

    def layer_weights(l):
        wg = jnp.zeros((LR_PAD, 2 * GLA_K), F32)
        wg = wg.at[0:GATE_RANK, 0:GLA_K].set(w_gate2[l, 0])
        wg = wg.at[GATE_RANK:2 * GATE_RANK, GLA_K:].set(w_gate2[l, 1]).astype(BF16)
        return dict(
            g1=row(g_norm1[l]), w1=cols1(w_in[l]).astype(BF16), b1=row(cols1(b_in[l])),
            wg=wg, bg=b_gate2[l].reshape(1, 2 * GLA_K),
            w2=cols2(w_in[l]).astype(BF16), b2=row(cols2(b_in[l])), ggla=row(g_gla[l]),
            wa=w_br_gla[l].astype(BF16), wb=w_br_fft[l].astype(BF16), cw=conv_w[l], cb=row(conv_b[l]),
            lng=row(conv_ln_g[l]), lnb=row(conv_ln_b[l]), wc=w_br_conv[l].astype(BF16), wo=w_out[l].astype(BF16),
            g2=row(g_norm2[l]), wu=w_up[l].astype(BF16), fw=ffn_conv_w[l], fb=row(ffn_conv_b[l]),
            wd=w_down[l].astype(BF16))

    def layer(x, pos_in, mod, s0, p, emit_state, final):
        res = _inproj(x, pos_in, mod, p["g1"], p["w1"], p["b1"], cs_hi, cs_lo, p["wg"], p["bg"], cum_f, cum_b)
        gq, v, dec, fa, fb, u0 = res[:6]
        if pos_in is not None:
            x = res[6]
        s0_f = None if s0 is None else s0[:, 0]
        s0_b = None if s0 is None else s0[:, 1]
        ob, s_b = _gla(gq, v, dec, s0_b, None, reverse=True, emit_state=emit_state)
        o, s_f = _gla(gq, v, dec, s0_f, ob, reverse=False, emit_state=emit_state)
        yf = _seq_dft(fa, fb, dft[x.shape[1]])
        x = _merge(x, mod, o, yf, u0, p["g1"], p["w2"], p["b2"], p["ggla"], p["wa"], p["wb"],
                   p["cw"], p["cb"], p["lng"], p["lnb"], p["wc"], p["wo"])
        x = _ffn(x, mod, p["g2"], p["wu"], p["fw"], p["fb"], p["wd"], row(g_final), final=final)
        return x, s_f, s_b

    xp, xs = x_prompt, x_sample
    new_states = []
    for l in range(depth):
        final = l == depth - 1
        p = layer_weights(l)
        xp, s_f, s_b = layer(xp, None, mod_all[l, 0:1], None, p, True, final)
        new_states.append(jnp.stack([s_f, s_b], axis=1))
        xs, _, _ = layer(xs, pos if l == 0 else None, mod_all[l, 1:1 + n_lat], state_gla[:, l], p, False, final)
    return xp, xs, jnp.stack(new_states, axis=1)
```
